```python
import math
import jax, jax.numpy as jnp
from jax import lax
import numpy as np

D_MODEL = 4096
BATCH = 4
SEQ = 2048
DEPTH = 4
DEC_BATCH = 8
DEC_SEQ = 8
PAST_LEN = 8192
PAGE_SIZE = 128

W_CONV = D_MODEL // 4
W_ATTN = D_MODEL // 2
W_POOL = D_MODEL // 4
HEAD_DIM = 128
N_HEADS = W_ATTN // (2 * HEAD_DIM)
CONV_W = 3
POOL_WINDOWS = (2, 4, 8, 16)
N_POOL_GROUPS = len(POOL_WINDOWS)
POOL_GC = W_POOL // N_POOL_GROUPS
MAX_WIN = max(POOL_WINDOWS)
D_FF = ((8 * D_MODEL + 3 * 256 - 1) // (3 * 256)) * 256
Q_BLOCK = 128
EPS = 1e-6
IN_SIZES = (W_CONV, W_CONV, W_CONV, W_ATTN, W_ATTN, W_ATTN, W_POOL, D_MODEL, D_MODEL, D_MODEL)
N_IN = sum(IN_SIZES)

kernel_name = 'hybrid_gated_conv_diffattn_pool_decoder_step'


def _rmsnorm(x, g):
    xf = x.astype(jnp.float32)
    y = xf * lax.rsqrt(jnp.mean(xf * xf, axis=-1, keepdims=True) + EPS)
    return (y * g.astype(jnp.float32)).astype(x.dtype)


def _alibi_slopes():
    return jnp.asarray(2.0 ** (-8.0 * np.arange(1, N_HEADS + 1) / N_HEADS), dtype=jnp.float32)


def _split_cols(z):
    offs = np.cumsum((0,) + IN_SIZES)
    return [z[..., int(offs[i]):int(offs[i + 1])] for i in range(len(IN_SIZES))]


def _short_conv(u, prev, w):
    T = u.shape[1]
    ext = jnp.concatenate([prev.astype(u.dtype), u], axis=1)
    y = ext[:, 0:T] * w[0]
    for j in range(1, CONV_W):
        y = y + ext[:, j:j + T] * w[j]
    return y, ext[:, T:]


def _multiscale_pool(u, prev, start_pos):
    B, T, C = u.shape
    P = MAX_WIN - 1
    ext = jnp.concatenate([prev.astype(u.dtype), u], axis=1)
    cs = jnp.cumsum(ext.astype(jnp.float32), axis=1)
    cs = jnp.concatenate([jnp.zeros((B, 1, C), jnp.float32), cs], axis=1)
    pos = start_pos + jnp.arange(T)
    means = []
    for g, w in enumerate(POOL_WINDOWS):
        sl = slice(g * POOL_GC, (g + 1) * POOL_GC)
        win_sum = cs[:, P + 1:P + 1 + T, sl] - cs[:, P + 1 - w:P + 1 - w + T, sl]
        cnt = jnp.minimum(pos + 1, w).astype(jnp.float32)
        means.append(win_sum / cnt[None, :, None])
    mean = jnp.concatenate(means, axis=-1)
    return (mean - u.astype(jnp.float32)).astype(u.dtype), ext[:, T:]


def _diff_attn_core(q, k, v, q_pos, k_pos, lam, slopes):
    s = jnp.einsum('bqhjd,bkhjd->bhjqk', q.astype(jnp.float32), k.astype(jnp.float32)) * (HEAD_DIM ** -0.5)
    dist = q_pos[:, None] - k_pos[None, :]
    s = s - slopes[None, :, None, None, None] * dist.astype(jnp.float32)
    s = jnp.where(dist >= 0, s, -jnp.inf)
    p = jax.nn.softmax(s, axis=-1)
    a = p[:, :, 0] - lam * p[:, :, 1]
    return jnp.einsum('bhqk,bkhe->bqhe', a, v.astype(jnp.float32))


def _diff_attention(q, k, v, q_pos, k_pos, lam, n_blocks):
    slopes = _alibi_slopes()
    if n_blocks == 1:
        return _diff_attn_core(q, k, v, q_pos, k_pos, lam, slopes)
    B, T, H, _, DH = q.shape
    qb = q.reshape(B, n_blocks, T // n_blocks, H, 2, DH).transpose(1, 0, 2, 3, 4, 5)
    pb = q_pos.reshape(n_blocks, T // n_blocks)
    ob = lax.map(lambda a: _diff_attn_core(a[0], k, v, a[1], k_pos, lam, slopes), (qb, pb))
    return ob.transpose(1, 0, 2, 3, 4).reshape(B, T, H, 2 * DH)


def _layer(x, c, layer_idx, past_k, past_v, conv_prev, pool_prev, start_pos,
           w_ada, b_ada, g_norm1, w_in, conv_w, lam_q1, lam_k1, lam_q2, lam_k2,
           g_subln, w_pool, pool_scale, w_proj_conv, w_proj_attn, w_proj_pool,
           w_out, g_norm2, w_ffn_gate, w_ffn_up, w_ffn_down):
    B, T, _ = x.shape
    mod = (jax.nn.silu(c) @ w_ada + b_ada).astype(x.dtype)
    sh1, sc1, gt1, sh2, sc2, gt2 = [m[:, None, :] for m in jnp.split(mod, 6, axis=-1)]

    h = _rmsnorm(x, g_norm1) * (1 + sc1) + sh1
    xa, ba, ca, q, k, v, uc, ga, gb, gc = _split_cols(h @ w_in)

    conv_y, conv_state = _short_conv(ca * xa, conv_prev, conv_w)
    y_conv = ba * conv_y

    k_rows = k.reshape(B, T, N_HEADS, 2 * HEAD_DIM)
    v_rows = v.reshape(B, T, N_HEADS, 2 * HEAD_DIM)
    if past_k is None:
        k_all, v_all = k_rows, v_rows
    else:
        k_all = jnp.concatenate([past_k.astype(x.dtype), k_rows], axis=1)
        v_all = jnp.concatenate([past_v.astype(x.dtype), v_rows], axis=1)
    q_pos = start_pos + jnp.arange(T)
    k_pos = jnp.arange(start_pos + T)
    lam_init = 0.8 - 0.6 * math.exp(-0.3 * layer_idx)
    lam = (jnp.exp(jnp.sum(lam_q1.astype(jnp.float32) * lam_k1.astype(jnp.float32)))
           - jnp.exp(jnp.sum(lam_q2.astype(jnp.float32) * lam_k2.astype(jnp.float32))) + lam_init)
    n_blocks = T // Q_BLOCK if (T % Q_BLOCK == 0 and T > Q_BLOCK) else 1
    o = _diff_attention(q.reshape(B, T, N_HEADS, 2, HEAD_DIM),
                        k_all.reshape(B, -1, N_HEADS, 2, HEAD_DIM), v_all,
                        q_pos, k_pos, lam, n_blocks)
    y_attn = (_rmsnorm(o, g_subln) * (1.0 - lam_init)).reshape(B, T, W_ATTN).astype(x.dtype)

    pooled, pool_state = _multiscale_pool(uc, pool_prev, start_pos)
    y_pool = jnp.einsum('btgc,gce->btge', pooled.reshape(B, T, N_POOL_GROUPS, POOL_GC),
                        w_pool).reshape(B, T, W_POOL) * pool_scale

    merged = (jax.nn.sigmoid(ga) * (y_conv @ w_proj_conv)
              + jax.nn.sigmoid(gb) * (y_attn @ w_proj_attn)
              + jax.nn.sigmoid(gc) * (y_pool @ w_proj_pool))
    x = x + gt1 * (merged @ w_out)

    h2 = _rmsnorm(x, g_norm2) * (1 + sc2) + sh2
    x = x + gt2 * ((jax.nn.silu(h2 @ w_ffn_gate) * (h2 @ w_ffn_up)) @ w_ffn_down)
    return x, k_rows, v_rows, conv_state, pool_state


def setup_inputs(seed: int = 0) -> dict:
    key = jax.random.key(seed)
    keys = jax.random.split(key, 32)

    def nrm(i, shape, scale):
        return jax.random.normal(keys[i], shape, jnp.float32) * scale

    n_pages = PAST_LEN // PAGE_SIZE
    n_pool_pages = (DEC_BATCH * n_pages * 5) // 4
    kv_shape = (DEPTH, n_pool_pages, PAGE_SIZE, N_HEADS, 2 * HEAD_DIM)
    perm = jax.random.permutation(keys[4], n_pool_pages)
    page_table = perm[:DEC_BATCH * n_pages].reshape(DEC_BATCH, n_pages).astype(jnp.int32)
    return {
        'x_prompt': nrm(0, (BATCH, SEQ, D_MODEL), 1.0),
        'x_sample': nrm(1, (DEC_BATCH, DEC_SEQ, D_MODEL), 1.0),
        'cache_k': nrm(2, kv_shape, 1.0),
        'cache_v': nrm(3, kv_shape, 1.0),
        'state_conv': nrm(5, (DEPTH, DEC_BATCH, CONV_W - 1, W_CONV), 1.0),
        'state_pool': nrm(6, (DEPTH, DEC_BATCH, MAX_WIN - 1, W_POOL), 1.0),
        'page_table': page_table,
        'c_prompt': nrm(7, (BATCH, D_MODEL), 1.0),
        'c_sample': nrm(8, (DEC_BATCH, D_MODEL), 1.0),
        'w_ada': nrm(9, (DEPTH, D_MODEL, 6 * D_MODEL), 0.5 * D_MODEL ** -0.5),
        'b_ada': nrm(10, (DEPTH, 6 * D_MODEL), 0.02),
        'g_norm1': 1.0 + nrm(11, (DEPTH, D_MODEL), 0.05),
        'w_in': nrm(12, (DEPTH, D_MODEL, N_IN), D_MODEL ** -0.5),
        'conv_w': nrm(13, (DEPTH, CONV_W, W_CONV), CONV_W ** -0.5),
        'lam_q1': nrm(14, (DEPTH, HEAD_DIM), 0.1),
        'lam_k1': nrm(15, (DEPTH, HEAD_DIM), 0.1),
        'lam_q2': nrm(16, (DEPTH, HEAD_DIM), 0.1),
        'lam_k2': nrm(17, (DEPTH, HEAD_DIM), 0.1),
        'g_subln': 1.0 + nrm(18, (DEPTH, 2 * HEAD_DIM), 0.05),
        'w_pool': nrm(19, (DEPTH, N_POOL_GROUPS, POOL_GC, POOL_GC), POOL_GC ** -0.5),
        'pool_scale': 1.0 + nrm(20, (DEPTH, W_POOL), 0.1),
        'w_proj_conv': nrm(21, (DEPTH, W_CONV, D_MODEL), W_CONV ** -0.5),
        'w_proj_attn': nrm(22, (DEPTH, W_ATTN, D_MODEL), W_ATTN ** -0.5),
        'w_proj_pool': nrm(23, (DEPTH, W_POOL, D_MODEL), W_POOL ** -0.5),
        'w_out': nrm(24, (DEPTH, D_MODEL, D_MODEL), D_MODEL ** -0.5),
        'g_norm2': 1.0 + nrm(25, (DEPTH, D_MODEL), 0.05),
        'w_ffn_gate': nrm(26, (DEPTH, D_MODEL, D_FF), D_MODEL ** -0.5),
        'w_ffn_up': nrm(27, (DEPTH, D_MODEL, D_FF), D_MODEL ** -0.5),
        'w_ffn_down': nrm(28, (DEPTH, D_FF, D_MODEL), D_FF ** -0.5),
        'g_final': 1.0 + nrm(29, (D_MODEL,), 0.05),
    }


def reference(x_prompt, x_sample, cache_k, cache_v, state_conv, state_pool, page_table,
              c_prompt, c_sample, w_ada, b_ada, g_norm1, w_in, conv_w, lam_q1, lam_k1,
              lam_q2, lam_k2, g_subln, w_pool, pool_scale, w_proj_conv, w_proj_attn,
              w_proj_pool, w_out, g_norm2, w_ffn_gate, w_ffn_up, w_ffn_down, g_final):
    dec_b, n_pages = page_table.shape
    past_len = n_pages * cache_k.shape[2]
    b_prompt = x_prompt.shape[0]
    xp, xs = x_prompt, x_sample
    kp, vp, ks, vs, cp, csm, pp, psm = [], [], [], [], [], [], [], []
    for l in range(DEPTH):
        lw = (w_ada[l], b_ada[l], g_norm1[l], w_in[l], conv_w[l], lam_q1[l], lam_k1[l],
              lam_q2[l], lam_k2[l], g_subln[l], w_pool[l], pool_scale[l], w_proj_conv[l],
              w_proj_attn[l], w_proj_pool[l], w_out[l], g_norm2[l], w_ffn_gate[l],
              w_ffn_up[l], w_ffn_down[l])
        conv0 = jnp.zeros((b_prompt, CONV_W - 1, W_CONV), xp.dtype)
        pool0 = jnp.zeros((b_prompt, MAX_WIN - 1, W_POOL), xp.dtype)
        xp, k_r, v_r, c_st, p_st = _layer(xp, c_prompt, l, None, None, conv0, pool0, 0, *lw)
        kp.append(k_r); vp.append(v_r); cp.append(c_st); pp.append(p_st)
        past_k = cache_k[l][page_table].reshape(dec_b, past_len, N_HEADS, 2 * HEAD_DIM)
        past_v = cache_v[l][page_table].reshape(dec_b, past_len, N_HEADS, 2 * HEAD_DIM)
        xs, k_r, v_r, c_st, p_st = _layer(xs, c_sample, l, past_k, past_v, state_conv[l],
                                          state_pool[l], past_len, *lw)
        ks.append(k_r); vs.append(v_r); csm.append(c_st); psm.append(p_st)
    y_prompt = _rmsnorm(xp, g_final)
    y_sample = _rmsnorm(xs, g_final)
    return (y_prompt, y_sample, jnp.stack(kp), jnp.stack(vp), jnp.stack(ks), jnp.stack(vs),
            jnp.stack(cp), jnp.stack(csm), jnp.stack(pp), jnp.stack(psm))
```

```python
import functools
import math

import numpy as np
import jax
import jax.numpy as jnp
from jax import lax
from jax.experimental import pallas as pl
from jax.experimental.pallas import tpu as pltpu

F32 = jnp.float32
BF16 = jnp.bfloat16

EPS = 1e-6
HEAD_DIM = 128
HEAD_W = 2 * HEAD_DIM
CONV_W = 3
POOL_WINDOWS = (2, 4, 8, 16)
MAX_WIN = max(POOL_WINDOWS)
N_MOD = 6
MOD_SHIFT1, MOD_SCALE1, MOD_GATE1, MOD_SHIFT2, MOD_SCALE2, MOD_GATE2 = range(N_MOD)

V7X_VMEM_LIMIT = 56 * 1024 * 1024
SUBLANES = 8


def _cparams(n_axes, vmem=V7X_VMEM_LIMIT):
    return pltpu.CompilerParams(dimension_semantics=("arbitrary",) * n_axes,
                                vmem_limit_bytes=vmem)


def _silu(x):
    return x * jax.nn.sigmoid(x)


def _log2(n):
    assert n > 0 and n & (n - 1) == 0, n
    return n.bit_length() - 1


def _ada_kernel(c_ref, w_ref, b_ref, o_ref):
    s = _silu(c_ref[...]).astype(BF16)
    acc = jnp.dot(s, w_ref[...].astype(BF16), preferred_element_type=F32)
    o_ref[...] = acc + b_ref[...]


def _ada_mod(c_all, w_ada, b_ada, tn=512):
    depth, d, n = w_ada.shape
    rows = c_all.shape[0]
    return pl.pallas_call(
        _ada_kernel,
        grid=(depth, n // tn),
        in_specs=[pl.BlockSpec((rows, d), lambda l, j: (0, 0)),
                  pl.BlockSpec((None, d, tn), lambda l, j: (l, 0, j)),
                  pl.BlockSpec((None, 1, tn), lambda l, j: (l, 0, j))],
        out_specs=pl.BlockSpec((None, rows, tn), lambda l, j: (l, 0, j)),
        out_shape=jax.ShapeDtypeStruct((depth, rows, n), F32),
        compiler_params=_cparams(2),
        name="ada_mod",
    )(c_all, w_ada, b_ada.reshape(depth, 1, n))


class _Mod:
    def __init__(self, arr, rows_per_seq):
        self.arr = arr
        self.rows_per_seq = rows_per_seq

    def spec(self, which, tm, tn, row_of, col_of):
        if self.rows_per_seq is None:
            return pl.BlockSpec((None, None, tm, tn),
                                lambda *a: (a[-1][0], which, row_of(*a), col_of(*a)))
        tiles_per_seq = self.rows_per_seq // tm
        return pl.BlockSpec((None, None, None, 1, tn),
                            lambda *a: (a[-1][0], which, row_of(*a) // tiles_per_seq, 0, col_of(*a)))


def _norm_mod_kernel(l_ref, x_ref, g_ref, sc_ref, sh_ref, o_ref):
    x = x_ref[...]
    y = x * lax.rsqrt(jnp.mean(x * x, axis=-1, keepdims=True) + EPS) * g_ref[...]
    o_ref[...] = (y * (1.0 + sc_ref[...]) + sh_ref[...]).astype(o_ref.dtype)


def _norm_mod(lidx, x, g, mod, which_scale, which_shift, tr, out_dtype):
    m, d = x.shape
    depth = g.shape[0]
    row_of = lambda i, l: i
    col_of = lambda i, l: 0
    return pl.pallas_call(
        _norm_mod_kernel,
        grid_spec=pltpu.PrefetchScalarGridSpec(
            num_scalar_prefetch=1, grid=(m // tr,),
            in_specs=[pl.BlockSpec((tr, d), lambda i, l: (i, 0)),
                      pl.BlockSpec((None, 1, d), lambda i, l: (l[0], 0, 0)),
                      mod.spec(which_scale, tr, d, row_of, col_of),
                      mod.spec(which_shift, tr, d, row_of, col_of)],
            out_specs=pl.BlockSpec((tr, d), lambda i, l: (i, 0))),
        out_shape=jax.ShapeDtypeStruct((m, d), out_dtype),
        compiler_params=_cparams(1),
        name="norm_mod",
    )(lidx, x, g.reshape(depth, 1, d), mod.arr, mod.arr)


def _final_norm_kernel(x_ref, g_ref, o_ref):
    x = x_ref[...]
    o_ref[...] = x * lax.rsqrt(jnp.mean(x * x, axis=-1, keepdims=True) + EPS) * g_ref[...]


def _final_norm(x, g, tr):
    m, d = x.shape
    return pl.pallas_call(
        _final_norm_kernel,
        grid=(m // tr,),
        in_specs=[pl.BlockSpec((tr, d), lambda i: (i, 0)),
                  pl.BlockSpec((1, d), lambda i: (0, 0))],
        out_specs=pl.BlockSpec((tr, d), lambda i: (i, 0)),
        out_shape=jax.ShapeDtypeStruct((m, d), F32),
        compiler_params=_cparams(1),
        name="final_norm",
    )(x, g.reshape(1, d))


def _cast_weights_once(pairs):
    @pl.when(pl.program_id(1) == 0)
    def _():
        for w_ref, wb_ref in pairs:
            wb_ref[...] = w_ref[...].astype(BF16)


def _mm_plain_kernel(l_ref, x_ref, w_ref, o_ref, wb_ref):
    _cast_weights_once([(w_ref, wb_ref)])
    o_ref[...] = jnp.dot(x_ref[...].astype(BF16), wb_ref[...],
                         preferred_element_type=F32).astype(o_ref.dtype)


def _mm_resid_kernel(l_ref, x_ref, w_ref, res_ref, gate_ref, o_ref, wb_ref):
    _cast_weights_once([(w_ref, wb_ref)])
    acc = jnp.dot(x_ref[...].astype(BF16), wb_ref[...], preferred_element_type=F32)
    o_ref[...] = res_ref[...] + gate_ref[...] * acc


def _mm_swiglu_kernel(l_ref, x_ref, wg_ref, wu_ref, o_ref, wgb_ref, wub_ref):
    _cast_weights_once([(wg_ref, wgb_ref), (wu_ref, wub_ref)])
    x = x_ref[...].astype(BF16)
    a = jnp.dot(x, wgb_ref[...], preferred_element_type=F32)
    b = jnp.dot(x, wub_ref[...], preferred_element_type=F32)
    o_ref[...] = (_silu(a) * b).astype(o_ref.dtype)


def _mm_proj_kernel(l_ref, yc_ref, ya_ref, yp_ref, wc_ref, wa_ref, wp_ref,
                    ga_ref, gb_ref, gc_ref, o_ref, wcb_ref, wab_ref, wpb_ref):
    _cast_weights_once([(wc_ref, wcb_ref), (wa_ref, wab_ref), (wp_ref, wpb_ref)])
    out = jax.nn.sigmoid(ga_ref[...]) * jnp.dot(yc_ref[...].astype(BF16), wcb_ref[...],
                                                preferred_element_type=F32)
    out += jax.nn.sigmoid(gb_ref[...]) * jnp.dot(ya_ref[...].astype(BF16), wab_ref[...],
                                                 preferred_element_type=F32)
    out += jax.nn.sigmoid(gc_ref[...]) * jnp.dot(yp_ref[...].astype(BF16), wpb_ref[...],
                                                 preferred_element_type=F32)
    o_ref[...] = out.astype(o_ref.dtype)


def _w_spec(k, tn, buffers):
    kw = {} if buffers == 2 else {"pipeline_mode": pl.Buffered(buffers)}
    return pl.BlockSpec((None, k, tn), lambda j, i, l: (l[0], 0, j), **kw)


def _mm_plain(lidx, x, w, tm, tn, out_dtype):
    m, k = x.shape
    n = w.shape[2]
    return pl.pallas_call(
        _mm_plain_kernel,
        grid_spec=pltpu.PrefetchScalarGridSpec(
            num_scalar_prefetch=1, grid=(n // tn, m // tm),
            in_specs=[pl.BlockSpec((tm, k), lambda j, i, l: (i, 0)), _w_spec(k, tn, 2)],
            out_specs=pl.BlockSpec((tm, tn), lambda j, i, l: (i, j)),
            scratch_shapes=[pltpu.VMEM((k, tn), BF16)]),
        out_shape=jax.ShapeDtypeStruct((m, n), out_dtype),
        compiler_params=_cparams(2),
        name="mm_plain",
    )(lidx, x, w)


def _mm_resid(lidx, x, w, res, mod, which_gate, tm, tn, w_buffers=2):
    m, k = x.shape
    n = w.shape[2]
    return pl.pallas_call(
        _mm_resid_kernel,
        grid_spec=pltpu.PrefetchScalarGridSpec(
            num_scalar_prefetch=1, grid=(n // tn, m // tm),
            in_specs=[pl.BlockSpec((tm, k), lambda j, i, l: (i, 0)), _w_spec(k, tn, w_buffers),
                      pl.BlockSpec((tm, tn), lambda j, i, l: (i, j)),
                      mod.spec(which_gate, tm, tn, lambda j, i, l: i, lambda j, i, l: j)],
            out_specs=pl.BlockSpec((tm, tn), lambda j, i, l: (i, j)),
            scratch_shapes=[pltpu.VMEM((k, tn), BF16)]),
        out_shape=jax.ShapeDtypeStruct((m, n), F32),
        compiler_params=_cparams(2),
        name="mm_resid",
    )(lidx, x, w, res, mod.arr)


def _mm_swiglu(lidx, x, wg, wu, tm, tn, out_dtype):
    m, k = x.shape
    n = wg.shape[2]
    return pl.pallas_call(
        _mm_swiglu_kernel,
        grid_spec=pltpu.PrefetchScalarGridSpec(
            num_scalar_prefetch=1, grid=(n // tn, m // tm),
            in_specs=[pl.BlockSpec((tm, k), lambda j, i, l: (i, 0)),
                      _w_spec(k, tn, 2), _w_spec(k, tn, 2)],
            out_specs=pl.BlockSpec((tm, tn), lambda j, i, l: (i, j)),
            scratch_shapes=[pltpu.VMEM((k, tn), BF16), pltpu.VMEM((k, tn), BF16)]),
        out_shape=jax.ShapeDtypeStruct((m, n), out_dtype),
        compiler_params=_cparams(2),
        name="mm_swiglu",
    )(lidx, x, wg, wu)


def _mm_proj(lidx, yc, ya, yp, wc, wa, wp, z, gate_cols, tm, tn, out_dtype):
    m = yc.shape[0]
    n = wc.shape[2]
    xs = lambda y: pl.BlockSpec((tm, y.shape[1]), lambda j, i, l: (i, 0))
    gs = lambda off: pl.BlockSpec((tm, tn), lambda j, i, l: (i, off // tn + j))
    return pl.pallas_call(
        _mm_proj_kernel,
        grid_spec=pltpu.PrefetchScalarGridSpec(
            num_scalar_prefetch=1, grid=(n // tn, m // tm),
            in_specs=[xs(yc), xs(ya), xs(yp),
                      _w_spec(wc.shape[1], tn, 2), _w_spec(wa.shape[1], tn, 2),
                      _w_spec(wp.shape[1], tn, 2),
                      gs(gate_cols[0]), gs(gate_cols[1]), gs(gate_cols[2])],
            out_specs=pl.BlockSpec((tm, tn), lambda j, i, l: (i, j)),
            scratch_shapes=[pltpu.VMEM((wc.shape[1], tn), BF16),
                            pltpu.VMEM((wa.shape[1], tn), BF16),
                            pltpu.VMEM((wp.shape[1], tn), BF16)]),
        out_shape=jax.ShapeDtypeStruct((m, n), out_dtype),
        compiler_params=_cparams(2),
        name="mm_proj",
    )(lidx, yc, ya, yp, wc, wa, wp, z, z, z)


_CONV_BASE = SUBLANES


def _conv_kernel(l_ref, xa_ref, ba_ref, ca_ref, prev_ref, w_ref, y_ref, st_ref, ext_ref):
    t = xa_ref.shape[0]
    p = CONV_W - 1
    u = ca_ref[...] * xa_ref[...]
    ext_ref[pl.ds(_CONV_BASE - p, p), :] = prev_ref[...]
    ext_ref[pl.ds(_CONV_BASE, t), :] = u
    w = w_ref[...]
    y = u * w[CONV_W - 1:CONV_W, :]
    for j in range(CONV_W - 1):
        y += ext_ref[pl.ds(_CONV_BASE - p + j, t), :] * w[j:j + 1, :]
    y_ref[...] = (ba_ref[...] * y).astype(y_ref.dtype)
    st_ref[...] = ext_ref[pl.ds(_CONV_BASE + t - p, p), :]


def _conv(lidx, z, prev, prev_has_layers, conv_w, n_seq, t, w_conv, cols, tc, out_dtype):
    depth = conv_w.shape[0]
    p = CONV_W - 1
    zs = lambda off: pl.BlockSpec((t, tc), lambda b, c, l: (b, off // tc + c))
    if prev_has_layers:
        prev_spec = pl.BlockSpec((None, None, p, tc), lambda b, c, l: (l[0], b, 0, c))
    else:
        prev_spec = pl.BlockSpec((None, None, p, tc), lambda b, c, l: (0, b, 0, c))
    return pl.pallas_call(
        _conv_kernel,
        grid_spec=pltpu.PrefetchScalarGridSpec(
            num_scalar_prefetch=1, grid=(n_seq, w_conv // tc),
            in_specs=[zs(cols[0]), zs(cols[1]), zs(cols[2]), prev_spec,
                      pl.BlockSpec((None, CONV_W, tc), lambda b, c, l: (l[0], 0, c))],
            out_specs=[pl.BlockSpec((t, tc), lambda b, c, l: (b, c)),
                       pl.BlockSpec((None, p, tc), lambda b, c, l: (b, 0, c))],
            scratch_shapes=[pltpu.VMEM((_CONV_BASE + t, tc), F32)]),
        out_shape=[jax.ShapeDtypeStruct((n_seq * t, w_conv), out_dtype),
                   jax.ShapeDtypeStruct((n_seq, p, w_conv), F32)],
        compiler_params=_cparams(2),
        name="short_conv",
    )(lidx, z, z, z, prev, conv_w)


_POOL_BASE = 2 * SUBLANES


def _pool_kernel(l_ref, u_ref, prev_ref, w_ref, sc_ref, y_ref, st_ref, ext_ref, *, start_pos):
    t = u_ref.shape[0]
    p = MAX_WIN - 1
    gc = w_ref.shape[1]
    ext_ref[pl.ds(_POOL_BASE - p, p), :] = prev_ref[...]
    ext_ref[pl.ds(_POOL_BASE, t), :] = u_ref[...]
    st_ref[...] = ext_ref[pl.ds(_POOL_BASE + t - p, p), :]
    pos = start_pos + lax.broadcasted_iota(jnp.int32, (t, 1), 0)
    for g, win in enumerate(POOL_WINDOWS):
        cols = slice(g * gc, (g + 1) * gc)
        u = ext_ref[pl.ds(_POOL_BASE, t), cols]
        win_sum = u
        for i in range(1, win):
            win_sum = win_sum + ext_ref[pl.ds(_POOL_BASE - i, t), cols]
        cnt = jnp.minimum(pos + 1, win).astype(F32)
        pooled = win_sum / cnt - u
        y = jnp.dot(pooled.astype(BF16), w_ref[g].astype(BF16), preferred_element_type=F32)
        y_ref[:, cols] = (y * sc_ref[:, cols]).astype(y_ref.dtype)


def _pool(lidx, z, prev, prev_has_layers, w_pool, pool_scale, n_seq, t, w_poolw, col, start_pos,
          out_dtype):
    depth, n_groups, gc, _ = w_pool.shape
    p = MAX_WIN - 1
    if prev_has_layers:
        prev_spec = pl.BlockSpec((None, None, p, w_poolw), lambda b, l: (l[0], b, 0, 0))
    else:
        prev_spec = pl.BlockSpec((None, None, p, w_poolw), lambda b, l: (0, b, 0, 0))
    return pl.pallas_call(
        functools.partial(_pool_kernel, start_pos=start_pos),
        grid_spec=pltpu.PrefetchScalarGridSpec(
            num_scalar_prefetch=1, grid=(n_seq,),
            in_specs=[pl.BlockSpec((t, w_poolw), lambda b, l: (b, col // w_poolw)),
                      prev_spec,
                      pl.BlockSpec((None, n_groups, gc, gc), lambda b, l: (l[0], 0, 0, 0)),
                      pl.BlockSpec((None, 1, w_poolw), lambda b, l: (l[0], 0, 0))],
            out_specs=[pl.BlockSpec((t, w_poolw), lambda b, l: (b, 0)),
                       pl.BlockSpec((None, p, w_poolw), lambda b, l: (b, 0, 0))],
            scratch_shapes=[pltpu.VMEM((_POOL_BASE + t, w_poolw), F32)]),
        out_shape=[jax.ShapeDtypeStruct((n_seq * t, w_poolw), out_dtype),
                   jax.ShapeDtypeStruct((n_seq, p, w_poolw), F32)],
        compiler_params=_cparams(1),
        name="ms_pool",
    )(lidx, z, prev, w_pool, pool_scale.reshape(depth, 1, w_poolw))


def _lam(lq1_ref, lk1_ref, lq2_ref, lk2_ref, lam_init):
    s1 = jnp.sum(lq1_ref[...] * lk1_ref[...], axis=-1, keepdims=True)
    s2 = jnp.sum(lq2_ref[...] * lk2_ref[...], axis=-1, keepdims=True)
    return jnp.exp(s1) - jnp.exp(s2) + lam_init


def _subln(o, g, lam_init):
    y = o * lax.rsqrt(jnp.mean(o * o, axis=-1, keepdims=True) + EPS) * g
    return y * (1.0 - lam_init)


def _attn_prompt_kernel(l_ref, q_ref, k_ref, v_ref, lq1_ref, lk1_ref, lq2_ref, lk2_ref,
                        li_ref, g_ref, slope_ref, o_ref, *, tq):
    qi = pl.program_id(2)
    scale = HEAD_DIM ** -0.5
    q = q_ref[...]
    qs = [(q[:, j * HEAD_DIM:(j + 1) * HEAD_DIM] * scale).astype(BF16) for j in range(2)]
    slope = slope_ref[:, 0:1]
    dist0 = (lax.broadcasted_iota(jnp.int32, (tq, tq), 0)
             - lax.broadcasted_iota(jnp.int32, (tq, tq), 1))

    def tile(ki, carry, diagonal):
        start = pl.multiple_of(ki * tq, tq)
        kt = k_ref[pl.ds(start, tq), :]
        vt = v_ref[pl.ds(start, tq), :].astype(BF16)
        dist = (dist0 + (qi - ki) * tq).astype(F32)
        bias = slope * dist
        out = []
        for j in range(2):
            m, l, acc = carry[j]
            kj = kt[:, j * HEAD_DIM:(j + 1) * HEAD_DIM].astype(BF16)
            s = lax.dot_general(qs[j], kj, (((1,), (1,)), ((), ())),
                                preferred_element_type=F32) - bias
            if diagonal:
                s = jnp.where(dist >= 0, s, -jnp.inf)
            m_new = jnp.maximum(m, jnp.max(s, axis=-1, keepdims=True))
            alpha = jnp.exp(m - m_new)
            pm = jnp.exp(s - m_new)
            l = alpha * l + jnp.sum(pm, axis=-1, keepdims=True)
            acc = alpha * acc + jnp.dot(pm.astype(BF16), vt, preferred_element_type=F32)
            out.append((m_new, l, acc))
        return tuple(out)

    init = tuple((jnp.full((tq, 1), -jnp.inf, F32), jnp.zeros((tq, 1), F32),
                  jnp.zeros((tq, HEAD_W), F32)) for _ in range(2))
    carry = lax.fori_loop(0, qi, lambda ki, c: tile(ki, c, False), init)
    (_, l1, a1), (_, l2, a2) = tile(qi, carry, True)

    lam_init = li_ref[:, 0:1]
    lam = _lam(lq1_ref, lk1_ref, lq2_ref, lk2_ref, lam_init)
    o = a1 / l1 - lam * (a2 / l2)
    o_ref[...] = _subln(o, g_ref[...], lam_init).astype(o_ref.dtype)


def _small_specs(n_grid):
    lay = lambda width: pl.BlockSpec((None, 1, width), lambda *a: (a[n_grid][0], 0, 0))
    return [lay(HEAD_DIM)] * 4 + [lay(HEAD_DIM), lay(HEAD_W)]


def _attn_prompt(lidx, z, lam_vecs, lam_init_rows, g_subln, slopes, n_seq, t, n_heads,
                 q_col, k_col, v_col, tq, out_dtype):
    nq = t // tq
    qb, kb, vb = q_col // HEAD_W, k_col // HEAD_W, v_col // HEAD_W
    return pl.pallas_call(
        functools.partial(_attn_prompt_kernel, tq=tq),
        grid_spec=pltpu.PrefetchScalarGridSpec(
            num_scalar_prefetch=1, grid=(n_seq, n_heads, nq),
            in_specs=[pl.BlockSpec((tq, HEAD_W), lambda b, h, i, l: (b * nq + i, qb + h)),
                      pl.BlockSpec((t, HEAD_W), lambda b, h, i, l: (b, kb + h)),
                      pl.BlockSpec((t, HEAD_W), lambda b, h, i, l: (b, vb + h))]
                     + _small_specs(3)
                     + [pl.BlockSpec((None, 1, HEAD_DIM), lambda b, h, i, l: (h, 0, 0))],
            out_specs=pl.BlockSpec((tq, HEAD_W), lambda b, h, i, l: (b * nq + i, h))),
        out_shape=jax.ShapeDtypeStruct((n_seq * t, n_heads * HEAD_W), out_dtype),
        compiler_params=_cparams(3),
        name="attn_prompt",
    )(lidx, z, z, z, *lam_vecs, lam_init_rows, g_subln, slopes)


def _attn_sample_kernel(l_ref, pt_ref, q_ref, k_ref, v_ref, kn_ref, vn_ref,
                        lq1_ref, lk1_ref, lq2_ref, lk2_ref, li_ref, g_ref, slope_ref,
                        o_ref, m_ref, l_ref_s, acc_ref, bias_ref, *, n_pages, page, n_heads, t_new):
    p = pl.program_id(1)
    scale = HEAD_DIM ** -0.5
    n_cols = q_ref.shape[1]
    per_head = 2 * t_new
    slope = slope_ref[...]
    past = n_pages * page

    def col_bcast(row):
        sq = jnp.transpose(jnp.broadcast_to(row, (n_cols, n_cols)))
        return jnp.concatenate([sq] * (HEAD_W // n_cols), axis=1)

    def ids(rows):
        r = lax.broadcasted_iota(jnp.int32, (rows, n_cols), 0)
        c = lax.broadcasted_iota(jnp.int32, (rows, n_cols), 1)
        return (r >> _log2(n_heads), r & (n_heads - 1), c >> _log2(per_head), c & (t_new - 1))

    @pl.when(p == 0)
    def _():
        m_ref[...] = jnp.full(m_ref.shape, -jnp.inf, F32)
        l_ref_s[...] = jnp.zeros(l_ref_s.shape, F32)
        acc_ref[...] = jnp.zeros(acc_ref.shape, F32)
        kpos, khead, chead, tok = ids(page * n_heads)
        dist = (past + tok - kpos).astype(F32)
        bias_ref[...] = jnp.where(khead == chead, -slope * dist, -jnp.inf)

    qb = (q_ref[...] * scale).astype(BF16)

    def update(s, vb):
        m_old = m_ref[...]
        m_new = jnp.maximum(m_old, jnp.max(s, axis=0, keepdims=True))
        alpha = jnp.exp(m_old - m_new)
        pm = jnp.exp(s - m_new)
        l_ref_s[...] = alpha * l_ref_s[...] + jnp.sum(pm, axis=0, keepdims=True)
        pv = lax.dot_general(pm.astype(BF16), vb, (((0,), (0,)), ((), ())),
                             preferred_element_type=F32)
        acc_ref[...] = acc_ref[...] * col_bcast(alpha) + pv
        m_ref[...] = m_new

    @pl.when(p < n_pages)
    def _():
        s = jnp.dot(k_ref[...].astype(BF16), qb, preferred_element_type=F32)
        page_start = jnp.full((1, n_cols), p * page, jnp.int32).astype(F32)
        s = s + bias_ref[...] + slope * page_start
        update(s, v_ref[...].astype(BF16))

    @pl.when(p == n_pages)
    def _():
        s = jnp.dot(kn_ref[...].astype(BF16), qb, preferred_element_type=F32)
        kpos, khead, chead, tok = ids(t_new * n_heads)
        valid = (khead == chead) & (kpos <= tok)
        s = jnp.where(valid, s - slope * (tok - kpos).astype(F32), -jnp.inf)
        update(s, vn_ref[...].astype(BF16))

        lam_init = li_ref[:, 0:1]
        lam = _lam(lq1_ref, lk1_ref, lq2_ref, lk2_ref, lam_init)
        acc = acc_ref[...] / col_bcast(l_ref_s[...])
        a3 = acc.reshape(n_heads, per_head, HEAD_W)
        o1 = a3[:, 0:t_new, :].reshape(n_heads * t_new, HEAD_W)
        o2 = a3[:, t_new:per_head, :].reshape(n_heads * t_new, HEAD_W)
        o_ref[...] = _subln(o1 - lam * o2, g_ref[...], lam_init)


def _attn_sample(lidx, page_table, qall, cache_k, cache_v, k_new, v_new, lam_vecs, lam_init_rows,
                 g_subln, slope_cols, t_new):
    n_seq, n_pages = page_table.shape
    rows = cache_k.shape[2]
    n_heads = k_new.shape[1] // t_new
    page = rows // n_heads
    n_cols = qall.shape[2]
    last = n_pages - 1
    cache_spec = pl.BlockSpec(
        (None, None, rows, HEAD_W),
        lambda b, p, l, pt: (l[0], pt[b * n_pages + jnp.minimum(p, last)], 0, 0))
    per_seq = lambda r, c: pl.BlockSpec((None, r, c), lambda b, p, l, pt: (b, 0, 0))
    return pl.pallas_call(
        functools.partial(_attn_sample_kernel, n_pages=n_pages, page=page, n_heads=n_heads,
                          t_new=t_new),
        grid_spec=pltpu.PrefetchScalarGridSpec(
            num_scalar_prefetch=2, grid=(n_seq, n_pages + 1),
            in_specs=[per_seq(HEAD_W, n_cols), cache_spec, cache_spec,
                      per_seq(t_new * n_heads, HEAD_W), per_seq(t_new * n_heads, HEAD_W)]
                     + _small_specs(2)
                     + [pl.BlockSpec((1, n_cols), lambda b, p, l, pt: (0, 0))],
            out_specs=per_seq(n_heads * t_new, HEAD_W),
            scratch_shapes=[pltpu.VMEM((1, n_cols), F32), pltpu.VMEM((1, n_cols), F32),
                            pltpu.VMEM((n_cols, HEAD_W), F32),
                            pltpu.VMEM((rows, n_cols), F32)]),
        out_shape=jax.ShapeDtypeStruct((n_seq, n_heads * t_new, HEAD_W), F32),
        compiler_params=_cparams(2),
        name="attn_sample",
    )(lidx, page_table.reshape(-1), qall, cache_k, cache_v, k_new, v_new, *lam_vecs,
      lam_init_rows, g_subln, slope_cols)


def kernel(x_prompt, x_sample, cache_k, cache_v, state_conv, state_pool, page_table, c_prompt,
           c_sample, w_ada, b_ada, g_norm1, w_in, conv_w, lam_q1, lam_k1, lam_q2, lam_k2, g_subln,
           w_pool, pool_scale, w_proj_conv, w_proj_attn, w_proj_pool, w_out, g_norm2, w_ffn_gate,
           w_ffn_up, w_ffn_down, g_final):
    n_p, seq, d = x_prompt.shape
    n_s, dec_seq, _ = x_sample.shape
    depth = w_ada.shape[0]
    w_conv = conv_w.shape[2]
    w_poolw = pool_scale.shape[1]
    n_heads = cache_k.shape[3]
    w_attn = n_heads * HEAD_W
    page = cache_k.shape[2]
    n_pages = page_table.shape[1]
    past_len = n_pages * page
    m_p, m_s = n_p * seq, n_s * dec_seq

    sizes = (w_conv, w_conv, w_conv, w_attn, w_attn, w_attn, w_poolw, d, d, d)
    offs = [int(o) for o in np.cumsum((0,) + sizes)]
    c_xa, c_ba, c_ca, c_q, c_k, c_v, c_u, c_ga, c_gb, c_gc = offs[:10]

    n_c = n_p + n_s
    pad = (-n_c) % SUBLANES
    c_all = jnp.concatenate([c_prompt, c_sample, jnp.zeros((pad, d), F32)], axis=0)
    mod = _ada_mod(c_all, w_ada, b_ada)
    mod_p = _Mod(mod[:, :n_p].reshape(depth, n_p, N_MOD, 1, d).transpose(0, 2, 1, 3, 4), seq)
    mod_s_rows = jnp.repeat(mod[:, n_p:n_c].reshape(depth, n_s, N_MOD, d).transpose(0, 2, 1, 3),
                            dec_seq, axis=2)
    mod_s = _Mod(mod_s_rows, None)

    slopes_np = 2.0 ** (-8.0 * np.arange(1, n_heads + 1) / n_heads)
    slopes = jnp.asarray(np.broadcast_to(slopes_np[:, None, None], (n_heads, 1, HEAD_DIM)), F32)
    slope_cols = jnp.asarray(np.repeat(slopes_np, 2 * dec_seq)[None, :], F32)
    lam_init_np = np.array([0.8 - 0.6 * math.exp(-0.3 * l) for l in range(depth)])
    lam_init_rows = jnp.asarray(np.broadcast_to(lam_init_np[:, None, None], (depth, 1, HEAD_DIM)),
                                F32)
    lam_vecs = [v.reshape(depth, 1, HEAD_DIM) for v in (lam_q1, lam_k1, lam_q2, lam_k2)]
    g_sub = g_subln.reshape(depth, 1, HEAD_W)

    conv0 = jnp.zeros((1, n_p, CONV_W - 1, w_conv), F32)
    pool0 = jnp.zeros((1, n_p, MAX_WIN - 1, w_poolw), F32)
    cache_k2 = cache_k.reshape(depth, cache_k.shape[1], page * n_heads, HEAD_W)
    cache_v2 = cache_v.reshape(depth, cache_v.shape[1], page * n_heads, HEAD_W)
    eye2 = jnp.eye(2, dtype=F32)

    xp = x_prompt.reshape(m_p, d)
    xs = x_sample.reshape(m_s, d)
    outs = {k: [] for k in ("kp", "vp", "ks", "vs", "cp", "cs", "pp", "ps")}

    tm_p = 1024
    for layer in range(depth):
        lidx = jnp.full((1,), layer, jnp.int32)

        h = _norm_mod(lidx, xp, g_norm1, mod_p, MOD_SCALE1, MOD_SHIFT1, 256, BF16)
        z = _mm_plain(lidx, h, w_in, tm_p, 512, F32)
        y_conv, c_st = _conv(lidx, z, conv0, False, conv_w, n_p, seq, w_conv,
                             (c_xa, c_ba, c_ca), 256, BF16)
        y_attn = _attn_prompt(lidx, z, lam_vecs, lam_init_rows, g_sub, slopes, n_p, seq, n_heads,
                              c_q, c_k, c_v, 256, BF16)
        y_pool, p_st = _pool(lidx, z, pool0, False, w_pool, pool_scale, n_p, seq, w_poolw, c_u, 0,
                             BF16)
        merged = _mm_proj(lidx, y_conv, y_attn, y_pool, w_proj_conv, w_proj_attn, w_proj_pool, z,
                          (c_ga, c_gb, c_gc), 512, 512, BF16)
        xp = _mm_resid(lidx, merged, w_out, xp, mod_p, MOD_GATE1, tm_p, 512)
        h2 = _norm_mod(lidx, xp, g_norm2, mod_p, MOD_SCALE2, MOD_SHIFT2, 256, BF16)
        act = _mm_swiglu(lidx, h2, w_ffn_gate, w_ffn_up, tm_p, 256, BF16)
        xp = _mm_resid(lidx, act, w_ffn_down, xp, mod_p, MOD_GATE2, 256, 512, w_buffers=1)
        outs["kp"].append(z[:, c_k:c_k + w_attn].reshape(n_p, seq, n_heads, HEAD_W))
        outs["vp"].append(z[:, c_v:c_v + w_attn].reshape(n_p, seq, n_heads, HEAD_W))
        outs["cp"].append(c_st)
        outs["pp"].append(p_st)

        h = _norm_mod(lidx, xs, g_norm1, mod_s, MOD_SCALE1, MOD_SHIFT1, m_s, F32)
        z = _mm_plain(lidx, h, w_in, m_s, 512, F32)
        y_conv, c_st = _conv(lidx, z, state_conv, True, conv_w, n_s, dec_seq, w_conv,
                             (c_xa, c_ba, c_ca), 256, F32)
        k_rows = z[:, c_k:c_k + w_attn].reshape(n_s, dec_seq, n_heads, HEAD_W)
        v_rows = z[:, c_v:c_v + w_attn].reshape(n_s, dec_seq, n_heads, HEAD_W)
        q5 = z[:, c_q:c_q + w_attn].reshape(n_s, dec_seq, n_heads, 2, HEAD_DIM)
        qt = q5.transpose(0, 3, 4, 2, 1)
        qall = (qt[:, :, :, :, None, :] * eye2[None, :, None, None, :, None]).reshape(
            n_s, HEAD_W, n_heads * 2 * dec_seq)
        o_s = _attn_sample(lidx, page_table, qall, cache_k2, cache_v2,
                           k_rows.reshape(n_s, dec_seq * n_heads, HEAD_W),
                           v_rows.reshape(n_s, dec_seq * n_heads, HEAD_W),
                           lam_vecs, lam_init_rows, g_sub, slope_cols, dec_seq)
        y_attn = o_s.reshape(n_s, n_heads, dec_seq, HEAD_W).transpose(0, 2, 1, 3).reshape(
            m_s, w_attn)
        y_pool, p_st = _pool(lidx, z, state_pool, True, w_pool, pool_scale, n_s, dec_seq, w_poolw,
                             c_u, past_len, F32)
        merged = _mm_proj(lidx, y_conv, y_attn, y_pool, w_proj_conv, w_proj_attn, w_proj_pool, z,
                          (c_ga, c_gb, c_gc), m_s, 512, F32)
        xs = _mm_resid(lidx, merged, w_out, xs, mod_s, MOD_GATE1, m_s, 512)
        h2 = _norm_mod(lidx, xs, g_norm2, mod_s, MOD_SCALE2, MOD_SHIFT2, m_s, F32)
        act = _mm_swiglu(lidx, h2, w_ffn_gate, w_ffn_up, m_s, 256, F32)
        xs = _mm_resid(lidx, act, w_ffn_down, xs, mod_s, MOD_GATE2, m_s, 256)
        outs["ks"].append(k_rows)
        outs["vs"].append(v_rows)
        outs["cs"].append(c_st)
        outs["ps"].append(p_st)

    y_prompt = _final_norm(xp, g_final, 256).reshape(n_p, seq, d)
    y_sample = _final_norm(xs, g_final, m_s).reshape(n_s, dec_seq, d)
    st = lambda k: jnp.stack(outs[k])
    return (y_prompt, y_sample, st("kp"), st("vp"), st("ks"), st("vs"),
            st("cp"), st("cs"), st("pp"), st("ps"))
```

```python
import functools
import math

import numpy as np
import jax
import jax.numpy as jnp
from jax import lax
from jax.experimental import pallas as pl
from jax.experimental.pallas import tpu as pltpu

F32 = jnp.float32
BF16 = jnp.bfloat16

EPS = 1e-6
HEAD_DIM = 128
HEAD_W = 2 * HEAD_DIM
CONV_W = 3
POOL_WINDOWS = (2, 4, 8, 16)
MAX_WIN = max(POOL_WINDOWS)
N_MOD = 6
MOD_SHIFT1, MOD_SCALE1, MOD_GATE1, MOD_SHIFT2, MOD_SCALE2, MOD_GATE2 = range(N_MOD)

V7X_VMEM_LIMIT = 56 * 1024 * 1024
SUBLANES = 8
LANES = 128


def _cparams(n_axes, vmem=V7X_VMEM_LIMIT):
    return pltpu.CompilerParams(dimension_semantics=("arbitrary",) * n_axes,
                                vmem_limit_bytes=vmem)


def _silu(x):
    return x * jax.nn.sigmoid(x)


def _log2(n):
    assert n > 0 and n & (n - 1) == 0, n
    return n.bit_length() - 1


def _fit(pref, *dims):
    g = 0
    for d in dims:
        g = math.gcd(g, d)
    best = None
    for t in range(LANES, min(pref, g) + 1, LANES):
        if g % t == 0:
            best = t
    assert best is not None, (pref, dims)
    return best


def _ada_kernel(c_ref, w_ref, b_ref, o_ref):
    s = _silu(c_ref[...]).astype(BF16)
    acc = jnp.dot(s, w_ref[...].astype(BF16), preferred_element_type=F32)
    o_ref[...] = acc + b_ref[...]


def _ada_mod(c_all, w_ada, b_ada):
    depth, d, n = w_ada.shape
    rows = c_all.shape[0]
    tn = _fit(512, n)
    return pl.pallas_call(
        _ada_kernel,
        grid=(depth, n // tn),
        in_specs=[pl.BlockSpec((rows, d), lambda l, j: (0, 0)),
                  pl.BlockSpec((None, d, tn), lambda l, j: (l, 0, j)),
                  pl.BlockSpec((None, 1, tn), lambda l, j: (l, 0, j))],
        out_specs=pl.BlockSpec((None, rows, tn), lambda l, j: (l, 0, j)),
        out_shape=jax.ShapeDtypeStruct((depth, rows, n), F32),
        compiler_params=_cparams(2),
        name="ada_mod",
    )(c_all, w_ada, b_ada.reshape(depth, 1, n))


class _Mod:
    def __init__(self, arr, rows_per_seq):
        self.arr = arr
        self.rows_per_seq = rows_per_seq

    def spec(self, which, tm, tn, row_of, col_of):
        if self.rows_per_seq is None:
            return pl.BlockSpec((None, None, tm, tn),
                                lambda *a: (a[-1][0], which, row_of(*a), col_of(*a)))
        tiles_per_seq = self.rows_per_seq // tm
        return pl.BlockSpec((None, None, None, 1, tn),
                            lambda *a: (a[-1][0], which, row_of(*a) // tiles_per_seq, 0, col_of(*a)))


def _norm_mod_kernel(l_ref, x_ref, g_ref, sc_ref, sh_ref, o_ref):
    x = x_ref[...]
    y = x * lax.rsqrt(jnp.mean(x * x, axis=-1, keepdims=True) + EPS) * g_ref[...]
    o_ref[...] = (y * (1.0 + sc_ref[...]) + sh_ref[...]).astype(o_ref.dtype)


def _norm_mod(lidx, x, g, mod, which_scale, which_shift, tr, out_dtype):
    m, d = x.shape
    depth = g.shape[0]
    row_of = lambda i, l: i
    col_of = lambda i, l: 0
    return pl.pallas_call(
        _norm_mod_kernel,
        grid_spec=pltpu.PrefetchScalarGridSpec(
            num_scalar_prefetch=1, grid=(m // tr,),
            in_specs=[pl.BlockSpec((tr, d), lambda i, l: (i, 0)),
                      pl.BlockSpec((None, 1, d), lambda i, l: (l[0], 0, 0)),
                      mod.spec(which_scale, tr, d, row_of, col_of),
                      mod.spec(which_shift, tr, d, row_of, col_of)],
            out_specs=pl.BlockSpec((tr, d), lambda i, l: (i, 0))),
        out_shape=jax.ShapeDtypeStruct((m, d), out_dtype),
        compiler_params=_cparams(1),
        name="norm_mod",
    )(lidx, x, g.reshape(depth, 1, d), mod.arr, mod.arr)


def _final_norm_kernel(x_ref, g_ref, o_ref):
    x = x_ref[...]
    o_ref[...] = x * lax.rsqrt(jnp.mean(x * x, axis=-1, keepdims=True) + EPS) * g_ref[...]


def _final_norm(x, g, tr):
    m, d = x.shape
    return pl.pallas_call(
        _final_norm_kernel,
        grid=(m // tr,),
        in_specs=[pl.BlockSpec((tr, d), lambda i: (i, 0)),
                  pl.BlockSpec((1, d), lambda i: (0, 0))],
        out_specs=pl.BlockSpec((tr, d), lambda i: (i, 0)),
        out_shape=jax.ShapeDtypeStruct((m, d), F32),
        compiler_params=_cparams(1),
        name="final_norm",
    )(x, g.reshape(1, d))


_MAX_W_CHUNKS = 8
_BF16_SUBLANES = 16


def _dense_kernel(*refs, n_x, w_of_x, n_extra, epilogue, n_chunks, nj):
    n_w = len(w_of_x)
    refs = refs[1:]
    x_refs, refs = refs[:n_x], refs[n_x:]
    w_refs, refs = refs[:n_w], refs[n_w:]
    extra, refs = refs[:n_extra], refs[n_extra:]
    o_ref, wb_refs = refs[0], refs[1:]
    jj = pl.program_id(0)
    i = pl.program_id(1)

    @pl.when((jj < nj) & (i < n_chunks))
    def _():
        for w_ref, wb_ref in zip(w_refs, wb_refs):
            rows = w_ref.shape[0]
            start = pl.multiple_of(i * rows, rows)
            wb_ref[jj % 2, pl.ds(start, rows), :] = w_ref[...].astype(BF16)

    @pl.when(jj >= 1)
    def _():
        slot = (jj - 1) % 2
        xb = [x_ref[...].astype(BF16) for x_ref in x_refs]
        acc = [jnp.dot(xb[w_of_x[k]], wb_refs[k][slot], preferred_element_type=F32)
               for k in range(n_w)]
        if epilogue == "plain":
            out = acc[0]
        elif epilogue == "swiglu":
            out = _silu(acc[0]) * acc[1]
        elif epilogue == "resid":
            out = extra[0][...] + extra[1][...] * acc[0]
        else:
            assert epilogue == "proj"
            out = jax.nn.sigmoid(extra[0][...].astype(F32)) * acc[0]
            for k in range(1, n_w):
                out += jax.nn.sigmoid(extra[k][...].astype(F32)) * acc[k]
        o_ref[...] = out.astype(o_ref.dtype)


def _dense(lidx, xs, ws, w_of_x, *, tm, tn, n_cols, out_dtype, epilogue="plain", w_col0=0,
           res=None, mod=None, which_gate=None, gates=()):
    m = xs[0][0].shape[0]
    ni = m // tm
    nj = pl.cdiv(n_cols, tn)
    n_chunks = min(ni, _MAX_W_CHUNKS)
    row = lambda jj, i, l: jnp.where(jj == 0, 0, i)
    col = lambda jj, i, l: jnp.maximum(jj - 1, 0)

    in_specs, operands, scratch = [], [], []
    for arr, k, cb in xs:
        in_specs.append(pl.BlockSpec((tm, k), lambda jj, i, l, cb=cb: (row(jj, i, l), cb)))
        operands.append(arr)
    for arr, k_off, k in ws:
        kc = k // n_chunks
        assert k % n_chunks == 0 and kc % _BF16_SUBLANES == 0 and k_off % kc == 0, (k, k_off)
        assert w_col0 % tn == 0, (w_col0, tn)
        in_specs.append(pl.BlockSpec(
            (None, kc, tn),
            lambda jj, i, l, r0=k_off // kc: (l[0], r0 + jnp.minimum(i, n_chunks - 1),
                                             w_col0 // tn + jnp.minimum(jj, nj - 1))))
        operands.append(arr)
        scratch.append(pltpu.VMEM((2, k, tn), BF16))
    n_extra = 0
    if epilogue == "resid":
        in_specs += [pl.BlockSpec((tm, tn), lambda jj, i, l: (row(jj, i, l), col(jj, i, l))),
                     mod.spec(which_gate, tm, tn, row, col)]
        operands += [res, mod.arr]
        n_extra = 2
    elif epilogue == "proj":
        for arr, off in gates:
            assert off % tn == 0, (off, tn)
            in_specs.append(pl.BlockSpec(
                (tm, tn), lambda jj, i, l, b=off // tn: (row(jj, i, l), b + col(jj, i, l))))
            operands.append(arr)
        n_extra = len(gates)

    return pl.pallas_call(
        functools.partial(_dense_kernel, n_x=len(xs), w_of_x=tuple(w_of_x), n_extra=n_extra,
                          epilogue=epilogue, n_chunks=n_chunks, nj=nj),
        grid_spec=pltpu.PrefetchScalarGridSpec(
            num_scalar_prefetch=1, grid=(nj + 1, ni),
            in_specs=in_specs,
            out_specs=pl.BlockSpec((tm, tn), lambda jj, i, l: (row(jj, i, l), col(jj, i, l))),
            scratch_shapes=scratch),
        out_shape=jax.ShapeDtypeStruct((m, n_cols), out_dtype),
        compiler_params=_cparams(2),
        name="dense_" + epilogue,
    )(lidx, *operands)


_CONV_BASE = SUBLANES


def _conv_kernel(l_ref, xa_ref, ba_ref, ca_ref, prev_ref, w_ref, y_ref, st_ref, ext_ref):
    t = xa_ref.shape[0]
    p = CONV_W - 1
    u = ca_ref[...].astype(F32) * xa_ref[...].astype(F32)
    ext_ref[pl.ds(_CONV_BASE - p, p), :] = prev_ref[...]
    ext_ref[pl.ds(_CONV_BASE, t), :] = u
    w = w_ref[...]
    y = u * w[CONV_W - 1:CONV_W, :]
    for j in range(CONV_W - 1):
        y += ext_ref[pl.ds(_CONV_BASE - p + j, t), :] * w[j:j + 1, :]
    y_ref[...] = (ba_ref[...].astype(F32) * y).astype(y_ref.dtype)
    st_ref[...] = ext_ref[pl.ds(_CONV_BASE + t - p, p), :]


def _conv(lidx, z, prev, prev_has_layers, conv_w, n_seq, t, w_conv, cols, tc, out_dtype):
    p = CONV_W - 1
    zs = lambda off: pl.BlockSpec((t, tc), lambda b, c, l: (b, off // tc + c))
    if prev_has_layers:
        prev_spec = pl.BlockSpec((None, None, p, tc), lambda b, c, l: (l[0], b, 0, c))
    else:
        prev_spec = pl.BlockSpec((None, None, p, tc), lambda b, c, l: (0, b, 0, c))
    return pl.pallas_call(
        _conv_kernel,
        grid_spec=pltpu.PrefetchScalarGridSpec(
            num_scalar_prefetch=1, grid=(n_seq, w_conv // tc),
            in_specs=[zs(cols[0]), zs(cols[1]), zs(cols[2]), prev_spec,
                      pl.BlockSpec((None, CONV_W, tc), lambda b, c, l: (l[0], 0, c))],
            out_specs=[pl.BlockSpec((t, tc), lambda b, c, l: (b, c)),
                       pl.BlockSpec((None, p, tc), lambda b, c, l: (b, 0, c))],
            scratch_shapes=[pltpu.VMEM((_CONV_BASE + t, tc), F32)]),
        out_shape=[jax.ShapeDtypeStruct((n_seq * t, w_conv), out_dtype),
                   jax.ShapeDtypeStruct((n_seq, p, w_conv), F32)],
        compiler_params=_cparams(2),
        name="short_conv",
    )(lidx, z, z, z, prev, conv_w)


_POOL_BASE = 2 * SUBLANES


def _pool_kernel(l_ref, u_ref, prev_ref, w_ref, sc_ref, y_ref, st_ref, ext_ref, *, start_pos):
    t = u_ref.shape[0]
    p = MAX_WIN - 1
    gc = w_ref.shape[1]
    ext_ref[pl.ds(_POOL_BASE - p, p), :] = prev_ref[...]
    ext_ref[pl.ds(_POOL_BASE, t), :] = u_ref[...]
    st_ref[...] = ext_ref[pl.ds(_POOL_BASE + t - p, p), :]
    pos = start_pos + lax.broadcasted_iota(jnp.int32, (t, 1), 0)
    for g, win in enumerate(POOL_WINDOWS):
        cols = slice(g * gc, (g + 1) * gc)
        u = ext_ref[pl.ds(_POOL_BASE, t), cols]
        win_sum = u
        for i in range(1, win):
            win_sum = win_sum + ext_ref[pl.ds(_POOL_BASE - i, t), cols]
        cnt = jnp.minimum(pos + 1, win).astype(F32)
        pooled = win_sum / cnt - u
        y = jnp.dot(pooled.astype(BF16), w_ref[g].astype(BF16), preferred_element_type=F32)
        y_ref[:, cols] = (y * sc_ref[:, cols]).astype(y_ref.dtype)


def _pool(lidx, u, prev, prev_has_layers, w_pool, pool_scale, n_seq, t, start_pos, out_dtype):
    depth, n_groups, gc, _ = w_pool.shape
    w_poolw = u.shape[1]
    p = MAX_WIN - 1
    if prev_has_layers:
        prev_spec = pl.BlockSpec((None, None, p, w_poolw), lambda b, l: (l[0], b, 0, 0))
    else:
        prev_spec = pl.BlockSpec((None, None, p, w_poolw), lambda b, l: (0, b, 0, 0))
    return pl.pallas_call(
        functools.partial(_pool_kernel, start_pos=start_pos),
        grid_spec=pltpu.PrefetchScalarGridSpec(
            num_scalar_prefetch=1, grid=(n_seq,),
            in_specs=[pl.BlockSpec((t, w_poolw), lambda b, l: (b, 0)),
                      prev_spec,
                      pl.BlockSpec((None, n_groups, gc, gc), lambda b, l: (l[0], 0, 0, 0)),
                      pl.BlockSpec((None, 1, w_poolw), lambda b, l: (l[0], 0, 0))],
            out_specs=[pl.BlockSpec((t, w_poolw), lambda b, l: (b, 0)),
                       pl.BlockSpec((None, p, w_poolw), lambda b, l: (b, 0, 0))],
            scratch_shapes=[pltpu.VMEM((_POOL_BASE + t, w_poolw), F32)]),
        out_shape=[jax.ShapeDtypeStruct((n_seq * t, w_poolw), out_dtype),
                   jax.ShapeDtypeStruct((n_seq, p, w_poolw), F32)],
        compiler_params=_cparams(1),
        name="ms_pool",
    )(lidx, u, prev, w_pool, pool_scale.reshape(depth, 1, w_poolw))


def _lam(lq1_ref, lk1_ref, lq2_ref, lk2_ref, lam_init):
    s1 = jnp.sum(lq1_ref[...] * lk1_ref[...], axis=-1, keepdims=True)
    s2 = jnp.sum(lq2_ref[...] * lk2_ref[...], axis=-1, keepdims=True)
    return jnp.exp(s1) - jnp.exp(s2) + lam_init


def _subln(o, g, lam_init):
    y = o * lax.rsqrt(jnp.mean(o * o, axis=-1, keepdims=True) + EPS) * g
    return y * (1.0 - lam_init)


def _attn_prompt_kernel(l_ref, q_ref, k_ref, v_ref, lq1_ref, lk1_ref, lq2_ref, lk2_ref,
                        li_ref, g_ref, slope_ref, o_ref, *, tq):
    qi = pl.program_id(2)
    scale = HEAD_DIM ** -0.5
    q = q_ref[...].astype(F32)
    qs = [(q[:, j * HEAD_DIM:(j + 1) * HEAD_DIM] * scale).astype(BF16) for j in range(2)]
    slope = slope_ref[:, 0:1]
    dist0 = (lax.broadcasted_iota(jnp.int32, (tq, tq), 0)
             - lax.broadcasted_iota(jnp.int32, (tq, tq), 1))

    def tile(ki, carry, diagonal):
        start = pl.multiple_of(ki * tq, tq)
        kt = k_ref[pl.ds(start, tq), :]
        vt = v_ref[pl.ds(start, tq), :].astype(BF16)
        dist = (dist0 + (qi - ki) * tq).astype(F32)
        bias = slope * dist
        out = []
        for j in range(2):
            m, l, acc = carry[j]
            kj = kt[:, j * HEAD_DIM:(j + 1) * HEAD_DIM].astype(BF16)
            s = lax.dot_general(qs[j], kj, (((1,), (1,)), ((), ())),
                                preferred_element_type=F32) - bias
            if diagonal:
                s = jnp.where(dist >= 0, s, -jnp.inf)
            m_new = jnp.maximum(m, jnp.max(s, axis=-1, keepdims=True))
            alpha = jnp.exp(m - m_new)
            pm = jnp.exp(s - m_new)
            l = alpha * l + jnp.sum(pm, axis=-1, keepdims=True)
            acc = alpha * acc + jnp.dot(pm.astype(BF16), vt, preferred_element_type=F32)
            out.append((m_new, l, acc))
        return tuple(out)

    init = tuple((jnp.full((tq, 1), -jnp.inf, F32), jnp.zeros((tq, 1), F32),
                  jnp.zeros((tq, HEAD_W), F32)) for _ in range(2))
    carry = lax.fori_loop(0, qi, lambda ki, c: tile(ki, c, False), init)
    (_, l1, a1), (_, l2, a2) = tile(qi, carry, True)

    lam_init = li_ref[:, 0:1]
    lam = _lam(lq1_ref, lk1_ref, lq2_ref, lk2_ref, lam_init)
    o = a1 / l1 - lam * (a2 / l2)
    o_ref[...] = _subln(o, g_ref[...], lam_init).astype(o_ref.dtype)


def _small_specs(n_grid):
    lay = lambda width: pl.BlockSpec((None, 1, width), lambda *a: (a[n_grid][0], 0, 0))
    return [lay(HEAD_DIM)] * 4 + [lay(HEAD_DIM), lay(HEAD_W)]


def _attn_prompt(lidx, zq, q_col, zk, zv, lam_vecs, lam_init_rows, g_subln, slopes,
                 n_seq, t, n_heads, tq, out_dtype):
    nq = t // tq
    qb = q_col // HEAD_W
    kv_spec = pl.BlockSpec((t, HEAD_W), lambda b, h, i, l: (b, h))
    return pl.pallas_call(
        functools.partial(_attn_prompt_kernel, tq=tq),
        grid_spec=pltpu.PrefetchScalarGridSpec(
            num_scalar_prefetch=1, grid=(n_seq, n_heads, nq),
            in_specs=[pl.BlockSpec((tq, HEAD_W), lambda b, h, i, l: (b * nq + i, qb + h)),
                      kv_spec, kv_spec]
                     + _small_specs(3)
                     + [pl.BlockSpec((None, 1, HEAD_DIM), lambda b, h, i, l: (h, 0, 0))],
            out_specs=pl.BlockSpec((tq, HEAD_W), lambda b, h, i, l: (b * nq + i, h))),
        out_shape=jax.ShapeDtypeStruct((n_seq * t, n_heads * HEAD_W), out_dtype),
        compiler_params=_cparams(3),
        name="attn_prompt",
    )(lidx, zq, zk, zv, *lam_vecs, lam_init_rows, g_subln, slopes)


def _attn_sample_kernel(l_ref, pt_ref, q_ref, k_ref, v_ref, kn_ref, vn_ref,
                        lq1_ref, lk1_ref, lq2_ref, lk2_ref, li_ref, g_ref, slope_ref,
                        o_ref, m_ref, l_ref_s, acc_ref, bias_ref, *, n_pages, page, n_heads, t_new):
    p = pl.program_id(1)
    scale = HEAD_DIM ** -0.5
    n_cols = q_ref.shape[1]
    per_head = 2 * t_new
    slope = slope_ref[...]
    past = n_pages * page

    def col_bcast(row):
        sq = jnp.transpose(jnp.broadcast_to(row, (n_cols, n_cols)))
        return jnp.concatenate([sq] * (HEAD_W // n_cols), axis=1)

    def ids(rows):
        r = lax.broadcasted_iota(jnp.int32, (rows, n_cols), 0)
        c = lax.broadcasted_iota(jnp.int32, (rows, n_cols), 1)
        return (r >> _log2(n_heads), r & (n_heads - 1), c >> _log2(per_head), c & (t_new - 1))

    @pl.when(p == 0)
    def _():
        m_ref[...] = jnp.full(m_ref.shape, -jnp.inf, F32)
        l_ref_s[...] = jnp.zeros(l_ref_s.shape, F32)
        acc_ref[...] = jnp.zeros(acc_ref.shape, F32)
        kpos, khead, chead, tok = ids(page * n_heads)
        dist = (past + tok - kpos).astype(F32)
        bias_ref[...] = jnp.where(khead == chead, -slope * dist, -jnp.inf)

    qb = (q_ref[...] * scale).astype(BF16)

    def update(s, vb):
        m_old = m_ref[...]
        m_new = jnp.maximum(m_old, jnp.max(s, axis=0, keepdims=True))
        alpha = jnp.exp(m_old - m_new)
        pm = jnp.exp(s - m_new)
        l_ref_s[...] = alpha * l_ref_s[...] + jnp.sum(pm, axis=0, keepdims=True)
        pv = lax.dot_general(pm.astype(BF16), vb, (((0,), (0,)), ((), ())),
                             preferred_element_type=F32)
        acc_ref[...] = acc_ref[...] * col_bcast(alpha) + pv
        m_ref[...] = m_new

    @pl.when(p < n_pages)
    def _():
        s = jnp.dot(k_ref[...].astype(BF16), qb, preferred_element_type=F32)
        page_start = jnp.full((1, n_cols), p * page, jnp.int32).astype(F32)
        s = s + bias_ref[...] + slope * page_start
        update(s, v_ref[...].astype(BF16))

    @pl.when(p == n_pages)
    def _():
        s = jnp.dot(kn_ref[...].astype(BF16), qb, preferred_element_type=F32)
        kpos, khead, chead, tok = ids(t_new * n_heads)
        valid = (khead == chead) & (kpos <= tok)
        s = jnp.where(valid, s - slope * (tok - kpos).astype(F32), -jnp.inf)
        update(s, vn_ref[...].astype(BF16))

        lam_init = li_ref[:, 0:1]
        lam = _lam(lq1_ref, lk1_ref, lq2_ref, lk2_ref, lam_init)
        acc = acc_ref[...] / col_bcast(l_ref_s[...])
        a3 = acc.reshape(n_heads, per_head, HEAD_W)
        o1 = a3[:, 0:t_new, :].reshape(n_heads * t_new, HEAD_W)
        o2 = a3[:, t_new:per_head, :].reshape(n_heads * t_new, HEAD_W)
        o_ref[...] = _subln(o1 - lam * o2, g_ref[...], lam_init)


def _attn_sample(lidx, page_table, qall, cache_k, cache_v, k_new, v_new, lam_vecs, lam_init_rows,
                 g_subln, slope_cols, t_new):
    n_seq, n_pages = page_table.shape
    rows = cache_k.shape[2]
    n_heads = k_new.shape[1] // t_new
    page = rows // n_heads
    n_cols = qall.shape[2]
    last = n_pages - 1
    cache_spec = pl.BlockSpec(
        (None, None, rows, HEAD_W),
        lambda b, p, l, pt: (l[0], pt[b * n_pages + jnp.minimum(p, last)], 0, 0))
    per_seq = lambda r, c: pl.BlockSpec((None, r, c), lambda b, p, l, pt: (b, 0, 0))
    return pl.pallas_call(
        functools.partial(_attn_sample_kernel, n_pages=n_pages, page=page, n_heads=n_heads,
                          t_new=t_new),
        grid_spec=pltpu.PrefetchScalarGridSpec(
            num_scalar_prefetch=2, grid=(n_seq, n_pages + 1),
            in_specs=[per_seq(HEAD_W, n_cols), cache_spec, cache_spec,
                      per_seq(t_new * n_heads, HEAD_W), per_seq(t_new * n_heads, HEAD_W)]
                     + _small_specs(2)
                     + [pl.BlockSpec((1, n_cols), lambda b, p, l, pt: (0, 0))],
            out_specs=per_seq(n_heads * t_new, HEAD_W),
            scratch_shapes=[pltpu.VMEM((1, n_cols), F32), pltpu.VMEM((1, n_cols), F32),
                            pltpu.VMEM((n_cols, HEAD_W), F32),
                            pltpu.VMEM((rows, n_cols), F32)]),
        out_shape=jax.ShapeDtypeStruct((n_seq, n_heads * t_new, HEAD_W), F32),
        compiler_params=_cparams(2),
        name="attn_sample",
    )(lidx, page_table.reshape(-1), qall, cache_k, cache_v, k_new, v_new, *lam_vecs,
      lam_init_rows, g_subln, slope_cols)


def kernel(x_prompt, x_sample, cache_k, cache_v, state_conv, state_pool, page_table, c_prompt,
           c_sample, w_ada, b_ada, g_norm1, w_in, conv_w, lam_q1, lam_k1, lam_q2, lam_k2, g_subln,
           w_pool, pool_scale, w_proj_conv, w_proj_attn, w_proj_pool, w_out, g_norm2, w_ffn_gate,
           w_ffn_up, w_ffn_down, g_final):
    n_p, seq, d = x_prompt.shape
    n_s, dec_seq, _ = x_sample.shape
    depth = w_ada.shape[0]
    w_conv = conv_w.shape[2]
    w_poolw = pool_scale.shape[1]
    n_heads = cache_k.shape[3]
    w_attn = n_heads * HEAD_W
    d_ff = w_ffn_gate.shape[2]
    page = cache_k.shape[2]
    n_pages = page_table.shape[1]
    past_len = n_pages * page
    m_p, m_s = n_p * seq, n_s * dec_seq

    wa = 3 * w_conv + w_attn
    off_k, off_v, off_u, off_g = wa, wa + w_attn, wa + 2 * w_attn, wa + 2 * w_attn + w_poolw
    c_xa, c_ba, c_ca, c_q = 0, w_conv, 2 * w_conv, 3 * w_conv

    tm_big, tm_mid = min(1024, m_p), min(512, m_p)
    tn_d = _fit(1024, d)
    tn_ff = 512
    half_ff = d_ff // 2
    assert d_ff % 2 == 0 and half_ff % LANES == 0
    tc = _fit(256, w_conv)
    tq = min(256, seq)
    tr = min(256, seq)

    n_c = n_p + n_s
    pad = (-n_c) % SUBLANES
    c_all = jnp.concatenate([c_prompt, c_sample, jnp.zeros((pad, d), F32)], axis=0)
    mod = _ada_mod(c_all, w_ada, b_ada)
    mod_p = _Mod(mod[:, :n_p].reshape(depth, n_p, N_MOD, 1, d).transpose(0, 2, 1, 3, 4), seq)
    mod_s_rows = jnp.repeat(mod[:, n_p:n_c].reshape(depth, n_s, N_MOD, d).transpose(0, 2, 1, 3),
                            dec_seq, axis=2)
    mod_s = _Mod(mod_s_rows, None)

    slopes_np = 2.0 ** (-8.0 * np.arange(1, n_heads + 1) / n_heads)
    slopes = jnp.asarray(np.broadcast_to(slopes_np[:, None, None], (n_heads, 1, HEAD_DIM)), F32)
    slope_cols = jnp.asarray(np.repeat(slopes_np, 2 * dec_seq)[None, :], F32)
    lam_init_np = np.array([0.8 - 0.6 * math.exp(-0.3 * l) for l in range(depth)])
    lam_init_rows = jnp.asarray(np.broadcast_to(lam_init_np[:, None, None], (depth, 1, HEAD_DIM)),
                                F32)
    lam_vecs = [v.reshape(depth, 1, HEAD_DIM) for v in (lam_q1, lam_k1, lam_q2, lam_k2)]
    g_sub = g_subln.reshape(depth, 1, HEAD_W)

    conv0 = jnp.zeros((1, n_p, CONV_W - 1, w_conv), F32)
    pool0 = jnp.zeros((1, n_p, MAX_WIN - 1, w_poolw), F32)
    cache_k2 = cache_k.reshape(depth, cache_k.shape[1], page * n_heads, HEAD_W)
    cache_v2 = cache_v.reshape(depth, cache_v.shape[1], page * n_heads, HEAD_W)
    eye2 = jnp.eye(2, dtype=F32)

    def in_proj(lidx, h, tm, tn_cap, dt):
        x = [(h, d, 0)]
        w = [(w_in, 0, d)]
        mk = lambda n, c0, dtype, unit: _dense(
            lidx, x, w, (0,), tm=tm, tn=_fit(tn_cap, unit, c0), n_cols=n, out_dtype=dtype,
            w_col0=c0)
        return (mk(wa, 0, dt, wa), mk(w_attn, off_k, F32, w_attn), mk(w_attn, off_v, F32, w_attn),
                mk(w_poolw, off_u, F32, w_poolw), mk(3 * d, off_g, dt, d))

    def out_proj_ffn(lidx, x, y_conv, y_attn, y_pool, zg, mod_g, tm_proj, tm_res, tm_ff, tn_dd,
                     tn_down, tn_ffw, dt):
        merged = _dense(lidx, [(y_conv, w_conv, 0), (y_attn, w_attn, 0), (y_pool, w_poolw, 0)],
                        [(w_proj_conv, 0, w_conv), (w_proj_attn, 0, w_attn),
                         (w_proj_pool, 0, w_poolw)], (0, 1, 2), tm=tm_proj, tn=tn_dd, n_cols=d,
                        out_dtype=dt, epilogue="proj", gates=[(zg, 0), (zg, d), (zg, 2 * d)])
        x = _dense(lidx, [(merged, d, 0)], [(w_out, 0, d)], (0,), tm=tm_res, tn=tn_dd, n_cols=d,
                   out_dtype=F32, epilogue="resid", res=x, mod=mod_g, which_gate=MOD_GATE1)
        h2 = _norm_mod(lidx, x, g_norm2, mod_g, MOD_SCALE2, MOD_SHIFT2, min(tr, x.shape[0]), dt)
        act = _dense(lidx, [(h2, d, 0)], [(w_ffn_gate, 0, d), (w_ffn_up, 0, d)], (0, 0),
                     tm=tm_ff, tn=tn_ffw, n_cols=d_ff, out_dtype=dt, epilogue="swiglu")
        for half in range(2):
            x = _dense(lidx, [(act, half_ff, half)], [(w_ffn_down, half * half_ff, half_ff)], (0,),
                       tm=tm_res, tn=tn_down, n_cols=d, out_dtype=F32, epilogue="resid", res=x,
                       mod=mod_g, which_gate=MOD_GATE2)
        return x

    xp = x_prompt.reshape(m_p, d)
    xs = x_sample.reshape(m_s, d)
    outs = {k: [] for k in ("kp", "vp", "ks", "vs", "cp", "cs", "pp", "ps")}

    for layer in range(depth):
        lidx = jnp.full((1,), layer, jnp.int32)

        h = _norm_mod(lidx, xp, g_norm1, mod_p, MOD_SCALE1, MOD_SHIFT1, tr, BF16)
        za, zk, zv, zu, zg = in_proj(lidx, h, tm_big, 1024, BF16)
        y_conv, c_st = _conv(lidx, za, conv0, False, conv_w, n_p, seq, w_conv,
                             (c_xa, c_ba, c_ca), tc, BF16)
        y_attn = _attn_prompt(lidx, za, c_q, zk, zv, lam_vecs, lam_init_rows,
                              g_sub, slopes, n_p, seq, n_heads, tq, BF16)
        y_pool, p_st = _pool(lidx, zu, pool0, False, w_pool, pool_scale, n_p, seq, 0, BF16)
        xp = out_proj_ffn(lidx, xp, y_conv, y_attn, y_pool, zg, mod_p, tm_mid, tm_mid, tm_big,
                          tn_d, tn_d, tn_ff, BF16)
        outs["kp"].append(zk.reshape(n_p, seq, n_heads, HEAD_W))
        outs["vp"].append(zv.reshape(n_p, seq, n_heads, HEAD_W))
        outs["cp"].append(c_st)
        outs["pp"].append(p_st)

        h = _norm_mod(lidx, xs, g_norm1, mod_s, MOD_SCALE1, MOD_SHIFT1, m_s, F32)
        za, zk, zv, zu, zg = in_proj(lidx, h, m_s, 512, F32)
        y_conv, c_st = _conv(lidx, za, state_conv, True, conv_w, n_s, dec_seq, w_conv,
                             (c_xa, c_ba, c_ca), tc, F32)
        k_rows = zk.reshape(n_s, dec_seq, n_heads, HEAD_W)
        v_rows = zv.reshape(n_s, dec_seq, n_heads, HEAD_W)
        q5 = za[:, c_q:c_q + w_attn].reshape(n_s, dec_seq, n_heads, 2, HEAD_DIM)
        qt = q5.transpose(0, 3, 4, 2, 1)
        qall = (qt[:, :, :, :, None, :] * eye2[None, :, None, None, :, None]).reshape(
            n_s, HEAD_W, n_heads * 2 * dec_seq)
        o_s = _attn_sample(lidx, page_table, qall, cache_k2, cache_v2,
                           k_rows.reshape(n_s, dec_seq * n_heads, HEAD_W),
                           v_rows.reshape(n_s, dec_seq * n_heads, HEAD_W),
                           lam_vecs, lam_init_rows, g_sub, slope_cols, dec_seq)
        y_attn = o_s.reshape(n_s, n_heads, dec_seq, HEAD_W).transpose(0, 2, 1, 3).reshape(
            m_s, w_attn)
        y_pool, p_st = _pool(lidx, zu, state_pool, True, w_pool, pool_scale, n_s, dec_seq,
                             past_len, F32)
        xs = out_proj_ffn(lidx, xs, y_conv, y_attn, y_pool, zg, mod_s, m_s, m_s, m_s,
                          min(tn_d, 512), min(tn_d, 256), _fit(256, d_ff), F32)
        outs["ks"].append(k_rows)
        outs["vs"].append(v_rows)
        outs["cs"].append(c_st)
        outs["ps"].append(p_st)

    y_prompt = _final_norm(xp, g_final, tr).reshape(n_p, seq, d)
    y_sample = _final_norm(xs, g_final, m_s).reshape(n_s, dec_seq, d)
    st = lambda k: jnp.stack(outs[k])
    return (y_prompt, y_sample, st("kp"), st("vp"), st("ks"), st("vs"),
            st("cp"), st("cs"), st("pp"), st("ps"))
```

```python
import functools
import math

import numpy as np
import jax
import jax.numpy as jnp
from jax import lax
from jax.experimental import pallas as pl
from jax.experimental.pallas import tpu as pltpu

F32 = jnp.float32
BF16 = jnp.bfloat16

EPS = 1e-6
HEAD_DIM = 128
HEAD_W = 2 * HEAD_DIM
CONV_W = 3
POOL_WINDOWS = (2, 4, 8, 16)
MAX_WIN = max(POOL_WINDOWS)
N_MOD = 6
MOD_SHIFT1, MOD_SCALE1, MOD_GATE1, MOD_SHIFT2, MOD_SCALE2, MOD_GATE2 = range(N_MOD)

V7X_VMEM_LIMIT = 56 * 1024 * 1024
SUBLANES = 8
LANES = 128


def _cparams(n_axes, vmem=V7X_VMEM_LIMIT):
    return pltpu.CompilerParams(dimension_semantics=("arbitrary",) * n_axes,
                                vmem_limit_bytes=vmem)


def _silu(x):
    return x * jax.nn.sigmoid(x)


def _log2(n):
    assert n > 0 and n & (n - 1) == 0, n
    return n.bit_length() - 1


def _fit(pref, *dims):
    g = 0
    for d in dims:
        g = math.gcd(g, d)
    best = None
    for t in range(LANES, min(pref, g) + 1, LANES):
        if g % t == 0:
            best = t
    assert best is not None, (pref, dims)
    return best


def _ada_kernel(c_ref, w_ref, b_ref, o_ref):
    s = _silu(c_ref[...]).astype(BF16)
    acc = jnp.dot(s, w_ref[...].astype(BF16), preferred_element_type=F32)
    o_ref[...] = acc + b_ref[...]


def _ada_mod(c_all, w_ada, b_ada):
    depth, d, n = w_ada.shape
    rows = c_all.shape[0]
    tn = _fit(512, n)
    return pl.pallas_call(
        _ada_kernel,
        grid=(depth, n // tn),
        in_specs=[pl.BlockSpec((rows, d), lambda l, j: (0, 0)),
                  pl.BlockSpec((None, d, tn), lambda l, j: (l, 0, j)),
                  pl.BlockSpec((None, 1, tn), lambda l, j: (l, 0, j))],
        out_specs=pl.BlockSpec((None, rows, tn), lambda l, j: (l, 0, j)),
        out_shape=jax.ShapeDtypeStruct((depth, rows, n), F32),
        compiler_params=_cparams(2),
        name="ada_mod",
    )(c_all, w_ada, b_ada.reshape(depth, 1, n))


class _Mod:
    def __init__(self, arr, rows_per_seq):
        self.arr = arr
        self.rows_per_seq = rows_per_seq

    def spec(self, which, tm, tn, row_of, col_of):
        if self.rows_per_seq is None:
            return pl.BlockSpec((None, None, tm, tn),
                                lambda *a: (a[-1][0], which, row_of(*a), col_of(*a)))
        tiles_per_seq = self.rows_per_seq // tm
        return pl.BlockSpec((None, None, None, 1, tn),
                            lambda *a: (a[-1][0], which, row_of(*a) // tiles_per_seq, 0, col_of(*a)))


def _norm_mod_kernel(l_ref, x_ref, g_ref, sc_ref, sh_ref, o_ref):
    x = x_ref[...]
    y = x * lax.rsqrt(jnp.mean(x * x, axis=-1, keepdims=True) + EPS) * g_ref[...]
    o_ref[...] = (y * (1.0 + sc_ref[...]) + sh_ref[...]).astype(o_ref.dtype)


def _norm_mod(lidx, x, g, mod, which_scale, which_shift, tr, out_dtype):
    m, d = x.shape
    depth = g.shape[0]
    row_of = lambda i, l: i
    col_of = lambda i, l: 0
    return pl.pallas_call(
        _norm_mod_kernel,
        grid_spec=pltpu.PrefetchScalarGridSpec(
            num_scalar_prefetch=1, grid=(m // tr,),
            in_specs=[pl.BlockSpec((tr, d), lambda i, l: (i, 0)),
                      pl.BlockSpec((None, 1, d), lambda i, l: (l[0], 0, 0)),
                      mod.spec(which_scale, tr, d, row_of, col_of),
                      mod.spec(which_shift, tr, d, row_of, col_of)],
            out_specs=pl.BlockSpec((tr, d), lambda i, l: (i, 0))),
        out_shape=jax.ShapeDtypeStruct((m, d), out_dtype),
        compiler_params=_cparams(1),
        name="norm_mod",
    )(lidx, x, g.reshape(depth, 1, d), mod.arr, mod.arr)


def _final_norm_kernel(x_ref, g_ref, o_ref):
    x = x_ref[...]
    o_ref[...] = x * lax.rsqrt(jnp.mean(x * x, axis=-1, keepdims=True) + EPS) * g_ref[...]


def _final_norm(x, g, tr):
    m, d = x.shape
    return pl.pallas_call(
        _final_norm_kernel,
        grid=(m // tr,),
        in_specs=[pl.BlockSpec((tr, d), lambda i: (i, 0)),
                  pl.BlockSpec((1, d), lambda i: (0, 0))],
        out_specs=pl.BlockSpec((tr, d), lambda i: (i, 0)),
        out_shape=jax.ShapeDtypeStruct((m, d), F32),
        compiler_params=_cparams(1),
        name="final_norm",
    )(x, g.reshape(1, d))


_MAX_W_CHUNKS = 8
_BF16_SUBLANES = 16


def _dense_kernel(*refs, n_groups, n_x, w_of_x, n_extra, epilogue, n_chunks, nj):
    n_w = len(w_of_x)
    refs = refs[1:]
    w_refs, refs = refs[:n_w], refs[n_w:]
    groups = []
    for _ in range(n_groups):
        groups.append((refs[:n_x], refs[n_x:n_x + n_extra]))
        refs = refs[n_x + n_extra:]
    o_refs, wb_refs = refs[:n_groups], refs[n_groups:]
    jj = pl.program_id(0)
    i = pl.program_id(1)

    @pl.when((jj < nj) & (i < n_chunks))
    def _():
        for w_ref, wb_ref in zip(w_refs, wb_refs):
            rows = w_ref.shape[0]
            start = pl.multiple_of(i * rows, rows)
            wb_ref[jj % 2, pl.ds(start, rows), :] = w_ref[...].astype(BF16)

    def compute(x_refs, extra, o_ref):
        slot = (jj - 1) % 2
        xb = [x_ref[...].astype(BF16) for x_ref in x_refs]
        acc = [jnp.dot(xb[w_of_x[k]], wb_refs[k][slot], preferred_element_type=F32)
               for k in range(n_w)]
        if epilogue == "plain":
            out = acc[0]
        elif epilogue == "swiglu":
            out = _silu(acc[0]) * acc[1]
        elif epilogue == "resid":
            out = extra[0][...] + extra[1][...] * acc[0]
        else:
            assert epilogue == "proj"
            out = jax.nn.sigmoid(extra[0][...].astype(F32)) * acc[0]
            for k in range(1, n_w):
                out += jax.nn.sigmoid(extra[k][...].astype(F32)) * acc[k]
        o_ref[...] = out.astype(o_ref.dtype)

    @pl.when(jj >= 1)
    def _():
        compute(*groups[0], o_refs[0])

    if n_groups == 2:
        @pl.when((jj >= 1) & (i == 0))
        def _():
            compute(*groups[1], o_refs[1])


def _dense(lidx, groups, ws, w_of_x, *, tn, n_cols, epilogue="plain", w_col0=0, which_gate=None):
    m = groups[0]["xs"][0][0].shape[0]
    ni = m // groups[0]["tm"]
    nj = pl.cdiv(n_cols, tn)
    n_chunks = min(ni, _MAX_W_CHUNKS)
    col = lambda jj, i, l: jnp.maximum(jj - 1, 0)

    in_specs, operands, scratch = [], [], []
    for arr, k_off, k in ws:
        kc = k // n_chunks
        assert k % n_chunks == 0 and kc % _BF16_SUBLANES == 0 and k_off % kc == 0, (k, k_off)
        assert w_col0 % tn == 0, (w_col0, tn)
        in_specs.append(pl.BlockSpec(
            (None, kc, tn),
            lambda jj, i, l, r0=k_off // kc: (l[0], r0 + jnp.minimum(i, n_chunks - 1),
                                             w_col0 // tn + jnp.minimum(jj, nj - 1))))
        operands.append(arr)
        scratch.append(pltpu.VMEM((2, k, tn), BF16))

    out_specs, out_shapes = [], []
    n_extra = {"plain": 0, "swiglu": 0, "resid": 2, "proj": len(groups[0].get("gates", ()))}[epilogue]
    for g, grp in enumerate(groups):
        tm = grp["tm"]
        mg = grp["xs"][0][0].shape[0]
        if g == 0:
            row = lambda jj, i, l: jnp.where(jj == 0, 0, i)
        else:
            assert mg == tm, (mg, tm)
            row = lambda jj, i, l: 0
        for arr, k, cb in grp["xs"]:
            in_specs.append(pl.BlockSpec(
                (tm, k), lambda jj, i, l, cb=cb, row=row: (row(jj, i, l), cb)))
            operands.append(arr)
        if epilogue == "resid":
            in_specs += [pl.BlockSpec((tm, tn), lambda jj, i, l, row=row: (row(jj, i, l),
                                                                           col(jj, i, l))),
                         grp["mod"].spec(which_gate, tm, tn, row, col)]
            operands += [grp["res"], grp["mod"].arr]
        elif epilogue == "proj":
            for arr, off in grp["gates"]:
                assert off % tn == 0, (off, tn)
                in_specs.append(pl.BlockSpec(
                    (tm, tn), lambda jj, i, l, b=off // tn, row=row: (row(jj, i, l),
                                                                      b + col(jj, i, l))))
                operands.append(arr)
        out_specs.append(pl.BlockSpec((tm, tn), lambda jj, i, l, row=row: (row(jj, i, l),
                                                                          col(jj, i, l))))
        out_shapes.append(jax.ShapeDtypeStruct((mg, n_cols), grp["out_dtype"]))

    return pl.pallas_call(
        functools.partial(_dense_kernel, n_groups=len(groups), n_x=len(groups[0]["xs"]),
                          w_of_x=tuple(w_of_x), n_extra=n_extra, epilogue=epilogue,
                          n_chunks=n_chunks, nj=nj),
        grid_spec=pltpu.PrefetchScalarGridSpec(
            num_scalar_prefetch=1, grid=(nj + 1, ni),
            in_specs=in_specs, out_specs=out_specs, scratch_shapes=scratch),
        out_shape=out_shapes,
        compiler_params=_cparams(2),
        name="dense_" + epilogue,
    )(lidx, *operands)


_CONV_BASE = SUBLANES


def _conv_kernel(l_ref, xa_ref, ba_ref, ca_ref, prev_ref, w_ref, y_ref, st_ref, ext_ref):
    t = xa_ref.shape[0]
    p = CONV_W - 1
    u = ca_ref[...].astype(F32) * xa_ref[...].astype(F32)
    ext_ref[pl.ds(_CONV_BASE - p, p), :] = prev_ref[...]
    ext_ref[pl.ds(_CONV_BASE, t), :] = u
    w = w_ref[...]
    y = u * w[CONV_W - 1:CONV_W, :]
    for j in range(CONV_W - 1):
        y += ext_ref[pl.ds(_CONV_BASE - p + j, t), :] * w[j:j + 1, :]
    y_ref[...] = (ba_ref[...].astype(F32) * y).astype(y_ref.dtype)
    st_ref[...] = ext_ref[pl.ds(_CONV_BASE + t - p, p), :]


def _conv(lidx, z, prev, prev_has_layers, conv_w, n_seq, t, w_conv, cols, tc, out_dtype):
    p = CONV_W - 1
    zs = lambda off: pl.BlockSpec((t, tc), lambda b, c, l: (b, off // tc + c))
    if prev_has_layers:
        prev_spec = pl.BlockSpec((None, None, p, tc), lambda b, c, l: (l[0], b, 0, c))
    else:
        prev_spec = pl.BlockSpec((None, None, p, tc), lambda b, c, l: (0, b, 0, c))
    return pl.pallas_call(
        _conv_kernel,
        grid_spec=pltpu.PrefetchScalarGridSpec(
            num_scalar_prefetch=1, grid=(n_seq, w_conv // tc),
            in_specs=[zs(cols[0]), zs(cols[1]), zs(cols[2]), prev_spec,
                      pl.BlockSpec((None, CONV_W, tc), lambda b, c, l: (l[0], 0, c))],
            out_specs=[pl.BlockSpec((t, tc), lambda b, c, l: (b, c)),
                       pl.BlockSpec((None, p, tc), lambda b, c, l: (b, 0, c))],
            scratch_shapes=[pltpu.VMEM((_CONV_BASE + t, tc), F32)]),
        out_shape=[jax.ShapeDtypeStruct((n_seq * t, w_conv), out_dtype),
                   jax.ShapeDtypeStruct((n_seq, p, w_conv), F32)],
        compiler_params=_cparams(2),
        name="short_conv",
    )(lidx, z, z, z, prev, conv_w)


_POOL_BASE = 2 * SUBLANES


def _pool_kernel(l_ref, u_ref, prev_ref, w_ref, sc_ref, y_ref, st_ref, ext_ref, *, start_pos):
    t = u_ref.shape[0]
    p = MAX_WIN - 1
    gc = w_ref.shape[1]
    ext_ref[pl.ds(_POOL_BASE - p, p), :] = prev_ref[...]
    ext_ref[pl.ds(_POOL_BASE, t), :] = u_ref[...]
    st_ref[...] = ext_ref[pl.ds(_POOL_BASE + t - p, p), :]
    pos = start_pos + lax.broadcasted_iota(jnp.int32, (t, 1), 0)
    for g, win in enumerate(POOL_WINDOWS):
        cols = slice(g * gc, (g + 1) * gc)
        u = ext_ref[pl.ds(_POOL_BASE, t), cols]
        win_sum = u
        for i in range(1, win):
            win_sum = win_sum + ext_ref[pl.ds(_POOL_BASE - i, t), cols]
        cnt = jnp.minimum(pos + 1, win).astype(F32)
        pooled = win_sum / cnt - u
        y = jnp.dot(pooled.astype(BF16), w_ref[g].astype(BF16), preferred_element_type=F32)
        y_ref[:, cols] = (y * sc_ref[:, cols]).astype(y_ref.dtype)


def _pool(lidx, u, prev, prev_has_layers, w_pool, pool_scale, n_seq, t, start_pos, out_dtype):
    depth, n_groups, gc, _ = w_pool.shape
    w_poolw = u.shape[1]
    p = MAX_WIN - 1
    if prev_has_layers:
        prev_spec = pl.BlockSpec((None, None, p, w_poolw), lambda b, l: (l[0], b, 0, 0))
    else:
        prev_spec = pl.BlockSpec((None, None, p, w_poolw), lambda b, l: (0, b, 0, 0))
    return pl.pallas_call(
        functools.partial(_pool_kernel, start_pos=start_pos),
        grid_spec=pltpu.PrefetchScalarGridSpec(
            num_scalar_prefetch=1, grid=(n_seq,),
            in_specs=[pl.BlockSpec((t, w_poolw), lambda b, l: (b, 0)),
                      prev_spec,
                      pl.BlockSpec((None, n_groups, gc, gc), lambda b, l: (l[0], 0, 0, 0)),
                      pl.BlockSpec((None, 1, w_poolw), lambda b, l: (l[0], 0, 0))],
            out_specs=[pl.BlockSpec((t, w_poolw), lambda b, l: (b, 0)),
                       pl.BlockSpec((None, p, w_poolw), lambda b, l: (b, 0, 0))],
            scratch_shapes=[pltpu.VMEM((_POOL_BASE + t, w_poolw), F32)]),
        out_shape=[jax.ShapeDtypeStruct((n_seq * t, w_poolw), out_dtype),
                   jax.ShapeDtypeStruct((n_seq, p, w_poolw), F32)],
        compiler_params=_cparams(1),
        name="ms_pool",
    )(lidx, u, prev, w_pool, pool_scale.reshape(depth, 1, w_poolw))


def _lam(lq1_ref, lk1_ref, lq2_ref, lk2_ref, lam_init):
    s1 = jnp.sum(lq1_ref[...] * lk1_ref[...], axis=-1, keepdims=True)
    s2 = jnp.sum(lq2_ref[...] * lk2_ref[...], axis=-1, keepdims=True)
    return jnp.exp(s1) - jnp.exp(s2) + lam_init


def _subln(o, g, lam_init):
    y = o * lax.rsqrt(jnp.mean(o * o, axis=-1, keepdims=True) + EPS) * g
    return y * (1.0 - lam_init)


def _attn_prompt_kernel(l_ref, q_ref, k_ref, v_ref, lq1_ref, lk1_ref, lq2_ref, lk2_ref,
                        li_ref, g_ref, slope_ref, o_ref, *, tq):
    qi = pl.program_id(2)
    scale = HEAD_DIM ** -0.5
    q = q_ref[...].astype(F32)
    qs = [(q[:, j * HEAD_DIM:(j + 1) * HEAD_DIM] * scale).astype(BF16) for j in range(2)]
    slope = slope_ref[:, 0:1]
    dist0 = (lax.broadcasted_iota(jnp.int32, (tq, tq), 0)
             - lax.broadcasted_iota(jnp.int32, (tq, tq), 1))

    def tile(ki, carry, diagonal):
        start = pl.multiple_of(ki * tq, tq)
        kt = k_ref[pl.ds(start, tq), :]
        vt = v_ref[pl.ds(start, tq), :].astype(BF16)
        dist = (dist0 + (qi - ki) * tq).astype(F32)
        bias = slope * dist
        out = []
        for j in range(2):
            m, l, acc = carry[j]
            kj = kt[:, j * HEAD_DIM:(j + 1) * HEAD_DIM].astype(BF16)
            s = lax.dot_general(qs[j], kj, (((1,), (1,)), ((), ())),
                                preferred_element_type=F32) - bias
            if diagonal:
                s = jnp.where(dist >= 0, s, -jnp.inf)
            m_new = jnp.maximum(m, jnp.max(s, axis=-1, keepdims=True))
            alpha = jnp.exp(m - m_new)
            pm = jnp.exp(s - m_new)
            l = alpha * l + jnp.sum(pm, axis=-1, keepdims=True)
            acc = alpha * acc + jnp.dot(pm.astype(BF16), vt, preferred_element_type=F32)
            out.append((m_new, l, acc))
        return tuple(out)

    init = tuple((jnp.full((tq, 1), -jnp.inf, F32), jnp.zeros((tq, 1), F32),
                  jnp.zeros((tq, HEAD_W), F32)) for _ in range(2))
    carry = lax.fori_loop(0, qi, lambda ki, c: tile(ki, c, False), init)
    (_, l1, a1), (_, l2, a2) = tile(qi, carry, True)

    lam_init = li_ref[:, 0:1]
    lam = _lam(lq1_ref, lk1_ref, lq2_ref, lk2_ref, lam_init)
    o = a1 / l1 - lam * (a2 / l2)
    o_ref[...] = _subln(o, g_ref[...], lam_init).astype(o_ref.dtype)


def _small_specs(n_grid):
    lay = lambda width: pl.BlockSpec((None, 1, width), lambda *a: (a[n_grid][0], 0, 0))
    return [lay(HEAD_DIM)] * 4 + [lay(HEAD_DIM), lay(HEAD_W)]


def _attn_prompt(lidx, zq, q_col, zk, zv, lam_vecs, lam_init_rows, g_subln, slopes,
                 n_seq, t, n_heads, tq, out_dtype):
    nq = t // tq
    qb = q_col // HEAD_W
    kv_spec = pl.BlockSpec((t, HEAD_W), lambda b, h, i, l: (b, h))
    return pl.pallas_call(
        functools.partial(_attn_prompt_kernel, tq=tq),
        grid_spec=pltpu.PrefetchScalarGridSpec(
            num_scalar_prefetch=1, grid=(n_seq, n_heads, nq),
            in_specs=[pl.BlockSpec((tq, HEAD_W), lambda b, h, i, l: (b * nq + i, qb + h)),
                      kv_spec, kv_spec]
                     + _small_specs(3)
                     + [pl.BlockSpec((None, 1, HEAD_DIM), lambda b, h, i, l: (h, 0, 0))],
            out_specs=pl.BlockSpec((tq, HEAD_W), lambda b, h, i, l: (b * nq + i, h))),
        out_shape=jax.ShapeDtypeStruct((n_seq * t, n_heads * HEAD_W), out_dtype),
        compiler_params=_cparams(3),
        name="attn_prompt",
    )(lidx, zq, zk, zv, *lam_vecs, lam_init_rows, g_subln, slopes)


_PAGES_PER_STEP = 2


def _attn_sample_kernel(l_ref, pt_ref, q_ref, *refs, n_pages, page, n_heads, t_new):
    kv_refs, refs = refs[:2 * _PAGES_PER_STEP], refs[2 * _PAGES_PER_STEP:]
    (kn_ref, vn_ref, lq1_ref, lk1_ref, lq2_ref, lk2_ref, li_ref, g_ref, slope_ref,
     o_ref, m_ref, ls_ref, acc_ref, bias_ref) = refs
    p = pl.program_id(1)
    n_steps = n_pages // _PAGES_PER_STEP
    scale = HEAD_DIM ** -0.5
    n_rows = q_ref.shape[0]
    per_head = 2 * t_new
    slope = slope_ref[...]
    past = n_pages * page

    def ids(n_keys):
        r = lax.broadcasted_iota(jnp.int32, (n_rows, n_keys), 0)
        c = lax.broadcasted_iota(jnp.int32, (n_rows, n_keys), 1)
        return (r >> _log2(per_head), r & (t_new - 1), c >> _log2(n_heads), c & (n_heads - 1))

    @pl.when(p == 0)
    def _():
        m_ref[...] = jnp.full(m_ref.shape, -jnp.inf, F32)
        ls_ref[...] = jnp.zeros(ls_ref.shape, F32)
        acc_ref[...] = jnp.zeros(acc_ref.shape, F32)
        rhead, tok, kpos, khead = ids(page * n_heads)
        dist = (past + tok - kpos).astype(F32)
        bias_ref[...] = jnp.where(rhead == khead, -slope * dist, -jnp.inf)

    qb = (q_ref[...] * scale).astype(BF16)

    def update(kb, vb, bias, c_row):
        s = lax.dot_general(qb, kb, (((1,), (1,)), ((), ())), preferred_element_type=F32) + bias
        m_old = m_ref[...]
        m_new = jnp.maximum(m_old, jnp.max(s, axis=-1, keepdims=True) + c_row)
        alpha = jnp.exp(m_old - m_new)
        pm = jnp.exp(s - (m_new - c_row))
        ls_ref[...] = alpha * ls_ref[...] + jnp.sum(pm, axis=-1, keepdims=True)
        acc_ref[...] = alpha * acc_ref[...] + jnp.dot(pm.astype(BF16), vb,
                                                     preferred_element_type=F32)
        m_ref[...] = m_new

    @pl.when(p < n_steps)
    def _():
        for u in range(_PAGES_PER_STEP):
            first_pos = jnp.full((1, 1), (p * _PAGES_PER_STEP + u) * page, jnp.int32).astype(F32)
            update(kv_refs[2 * u][...].astype(BF16), kv_refs[2 * u + 1][...].astype(BF16),
                   bias_ref[...], slope * first_pos)

    @pl.when(p == n_steps)
    def _():
        rhead, tok, kpos, khead = ids(t_new * n_heads)
        valid = (rhead == khead) & (kpos <= tok)
        bias = jnp.where(valid, -slope * (tok - kpos).astype(F32), -jnp.inf)
        update(kn_ref[...].astype(BF16), vn_ref[...].astype(BF16), bias,
               jnp.zeros((1, 1), F32))

        lam_init = li_ref[:, 0:1]
        lam = _lam(lq1_ref, lk1_ref, lq2_ref, lk2_ref, lam_init)
        acc = acc_ref[...] / ls_ref[...]
        a3 = acc.reshape(n_heads, per_head, HEAD_W)
        o1 = a3[:, 0:t_new, :].reshape(n_heads * t_new, HEAD_W)
        o2 = a3[:, t_new:per_head, :].reshape(n_heads * t_new, HEAD_W)
        o_ref[...] = _subln(o1 - lam * o2, g_ref[...], lam_init)


def _attn_sample(lidx, page_table, qall, cache_k, cache_v, k_new, v_new, lam_vecs, lam_init_rows,
                 g_subln, slope_rows, t_new):
    n_seq, n_pages = page_table.shape
    keys = cache_k.shape[2]
    n_heads = k_new.shape[1] // t_new
    page = keys // n_heads
    n_rows = qall.shape[1]
    assert n_pages % _PAGES_PER_STEP == 0
    n_steps = n_pages // _PAGES_PER_STEP

    def cache_spec(u):
        return pl.BlockSpec(
            (None, None, keys, HEAD_W),
            lambda b, p, l, pt: (l[0], pt[b * n_pages + jnp.minimum(p, n_steps - 1)
                                          * _PAGES_PER_STEP + u], 0, 0))

    cache_specs, cache_args = [], []
    for u in range(_PAGES_PER_STEP):
        cache_specs += [cache_spec(u), cache_spec(u)]
        cache_args += [cache_k, cache_v]
    per_seq = lambda r, c: pl.BlockSpec((None, r, c), lambda b, p, l, pt: (b, 0, 0))
    return pl.pallas_call(
        functools.partial(_attn_sample_kernel, n_pages=n_pages, page=page, n_heads=n_heads,
                          t_new=t_new),
        grid_spec=pltpu.PrefetchScalarGridSpec(
            num_scalar_prefetch=2, grid=(n_seq, n_steps + 1),
            in_specs=[per_seq(n_rows, HEAD_W)] + cache_specs
                     + [per_seq(t_new * n_heads, HEAD_W), per_seq(t_new * n_heads, HEAD_W)]
                     + _small_specs(2)
                     + [pl.BlockSpec((n_rows, 1), lambda b, p, l, pt: (0, 0))],
            out_specs=per_seq(n_heads * t_new, HEAD_W),
            scratch_shapes=[pltpu.VMEM((n_rows, 1), F32), pltpu.VMEM((n_rows, 1), F32),
                            pltpu.VMEM((n_rows, HEAD_W), F32),
                            pltpu.VMEM((n_rows, keys), F32)]),
        out_shape=jax.ShapeDtypeStruct((n_seq, n_heads * t_new, HEAD_W), F32),
        compiler_params=_cparams(2),
        name="attn_sample",
    )(lidx, page_table.reshape(-1), qall, *cache_args, k_new, v_new, *lam_vecs,
      lam_init_rows, g_subln, slope_rows)


def kernel(x_prompt, x_sample, cache_k, cache_v, state_conv, state_pool, page_table, c_prompt,
           c_sample, w_ada, b_ada, g_norm1, w_in, conv_w, lam_q1, lam_k1, lam_q2, lam_k2, g_subln,
           w_pool, pool_scale, w_proj_conv, w_proj_attn, w_proj_pool, w_out, g_norm2, w_ffn_gate,
           w_ffn_up, w_ffn_down, g_final):
    n_p, seq, d = x_prompt.shape
    n_s, dec_seq, _ = x_sample.shape
    depth = w_ada.shape[0]
    w_conv = conv_w.shape[2]
    w_poolw = pool_scale.shape[1]
    n_heads = cache_k.shape[3]
    w_attn = n_heads * HEAD_W
    d_ff = w_ffn_gate.shape[2]
    page = cache_k.shape[2]
    n_pages = page_table.shape[1]
    past_len = n_pages * page
    m_p, m_s = n_p * seq, n_s * dec_seq

    wa = 3 * w_conv + w_attn
    off_k, off_v, off_u, off_g = wa, wa + w_attn, wa + 2 * w_attn, wa + 2 * w_attn + w_poolw
    c_xa, c_ba, c_ca, c_q = 0, w_conv, 2 * w_conv, 3 * w_conv

    tm_big, tm_mid = min(1024, m_p), min(512, m_p)
    tn_d = _fit(1024, d)
    tn_ff = 512
    half_ff = d_ff // 2
    assert d_ff % 2 == 0 and half_ff % LANES == 0
    tc = _fit(256, w_conv)
    tq = min(256, seq)
    tr = min(256, seq)

    n_c = n_p + n_s
    pad = (-n_c) % SUBLANES
    c_all = jnp.concatenate([c_prompt, c_sample, jnp.zeros((pad, d), F32)], axis=0)
    mod = _ada_mod(c_all, w_ada, b_ada)
    mod_p = _Mod(mod[:, :n_p].reshape(depth, n_p, N_MOD, 1, d).transpose(0, 2, 1, 3, 4), seq)
    mod_s_rows = jnp.repeat(mod[:, n_p:n_c].reshape(depth, n_s, N_MOD, d).transpose(0, 2, 1, 3),
                            dec_seq, axis=2)
    mod_s = _Mod(mod_s_rows, None)

    slopes_np = 2.0 ** (-8.0 * np.arange(1, n_heads + 1) / n_heads)
    slopes = jnp.asarray(np.broadcast_to(slopes_np[:, None, None], (n_heads, 1, HEAD_DIM)), F32)
    slope_rows = jnp.asarray(np.repeat(slopes_np, 2 * dec_seq)[:, None], F32)
    lam_init_np = np.array([0.8 - 0.6 * math.exp(-0.3 * l) for l in range(depth)])
    lam_init_rows = jnp.asarray(np.broadcast_to(lam_init_np[:, None, None], (depth, 1, HEAD_DIM)),
                                F32)
    lam_vecs = [v.reshape(depth, 1, HEAD_DIM) for v in (lam_q1, lam_k1, lam_q2, lam_k2)]
    g_sub = g_subln.reshape(depth, 1, HEAD_W)

    conv0 = jnp.zeros((1, n_p, CONV_W - 1, w_conv), F32)
    pool0 = jnp.zeros((1, n_p, MAX_WIN - 1, w_poolw), F32)
    cache_k2 = cache_k.reshape(depth, cache_k.shape[1], page * n_heads, HEAD_W)
    cache_v2 = cache_v.reshape(depth, cache_v.shape[1], page * n_heads, HEAD_W)
    eye2 = jnp.eye(2, dtype=F32)

    def both(xs_p, xs_s, tm, dt_p, dt_s, **kw):
        return [dict(xs=xs_p, tm=tm, out_dtype=dt_p, **{k: v[0] for k, v in kw.items()}),
                dict(xs=xs_s, tm=m_s, out_dtype=dt_s, **{k: v[1] for k, v in kw.items()})]

    def in_proj(lidx, h_p, h_s):
        w = [(w_in, 0, d)]
        mk = lambda n, c0, dt_p, unit: _dense(
            lidx, both([(h_p, d, 0)], [(h_s, d, 0)], tm_big, dt_p, F32), w, (0,),
            tn=_fit(1024, unit, c0), n_cols=n, w_col0=c0)
        return (mk(wa, 0, BF16, wa), mk(w_attn, off_k, F32, w_attn), mk(w_attn, off_v, F32, w_attn),
                mk(w_poolw, off_u, F32, w_poolw), mk(3 * d, off_g, BF16, d))

    def out_proj_ffn(lidx, x, ys, zg):
        y_xs = lambda g: [(ys[0][g], w_conv, 0), (ys[1][g], w_attn, 0), (ys[2][g], w_poolw, 0)]
        gate_cols = lambda g: [(zg[g], 0), (zg[g], d), (zg[g], 2 * d)]
        merged = _dense(lidx, both(y_xs(0), y_xs(1), tm_mid, BF16, BF16,
                                   gates=(gate_cols(0), gate_cols(1))),
                        [(w_proj_conv, 0, w_conv), (w_proj_attn, 0, w_attn),
                         (w_proj_pool, 0, w_poolw)], (0, 1, 2), tn=tn_d, n_cols=d, epilogue="proj")
        x = _dense(lidx, both([(merged[0], d, 0)], [(merged[1], d, 0)], tm_mid, F32, F32,
                              res=x, mod=(mod_p, mod_s)),
                   [(w_out, 0, d)], (0,), tn=tn_d, n_cols=d, epilogue="resid",
                   which_gate=MOD_GATE1)
        h2 = (_norm_mod(lidx, x[0], g_norm2, mod_p, MOD_SCALE2, MOD_SHIFT2, tr, BF16),
              _norm_mod(lidx, x[1], g_norm2, mod_s, MOD_SCALE2, MOD_SHIFT2, m_s, BF16))
        act = _dense(lidx, both([(h2[0], d, 0)], [(h2[1], d, 0)], tm_big, BF16, BF16),
                     [(w_ffn_gate, 0, d), (w_ffn_up, 0, d)], (0, 0), tn=tn_ff, n_cols=d_ff,
                     epilogue="swiglu")
        for half in range(2):
            x = _dense(lidx, both([(act[0], half_ff, half)], [(act[1], half_ff, half)], tm_mid,
                                  F32, F32, res=x, mod=(mod_p, mod_s)),
                       [(w_ffn_down, half * half_ff, half_ff)], (0,), tn=tn_d, n_cols=d,
                       epilogue="resid", which_gate=MOD_GATE2)
        return x

    x = (x_prompt.reshape(m_p, d), x_sample.reshape(m_s, d))
    outs = {k: [] for k in ("kp", "vp", "ks", "vs", "cp", "cs", "pp", "ps")}

    for layer in range(depth):
        lidx = jnp.full((1,), layer, jnp.int32)
        h_p = _norm_mod(lidx, x[0], g_norm1, mod_p, MOD_SCALE1, MOD_SHIFT1, tr, BF16)
        h_s = _norm_mod(lidx, x[1], g_norm1, mod_s, MOD_SCALE1, MOD_SHIFT1, m_s, BF16)
        za, zk, zv, zu, zg = in_proj(lidx, h_p, h_s)

        y_conv_p, c_st = _conv(lidx, za[0], conv0, False, conv_w, n_p, seq, w_conv,
                               (c_xa, c_ba, c_ca), tc, BF16)
        y_attn_p = _attn_prompt(lidx, za[0], c_q, zk[0], zv[0], lam_vecs, lam_init_rows,
                                g_sub, slopes, n_p, seq, n_heads, tq, BF16)
        y_pool_p, p_st = _pool(lidx, zu[0], pool0, False, w_pool, pool_scale, n_p, seq, 0, BF16)
        outs["kp"].append(zk[0].reshape(n_p, seq, n_heads, HEAD_W))
        outs["vp"].append(zv[0].reshape(n_p, seq, n_heads, HEAD_W))
        outs["cp"].append(c_st)
        outs["pp"].append(p_st)

        y_conv_s, c_st = _conv(lidx, za[1], state_conv, True, conv_w, n_s, dec_seq, w_conv,
                               (c_xa, c_ba, c_ca), tc, F32)
        k_rows = zk[1].reshape(n_s, dec_seq, n_heads, HEAD_W)
        v_rows = zv[1].reshape(n_s, dec_seq, n_heads, HEAD_W)
        q5 = za[1][:, c_q:c_q + w_attn].reshape(n_s, dec_seq, n_heads, 2, HEAD_DIM)
        qh = q5.transpose(0, 2, 3, 1, 4)
        qall = (qh[:, :, :, :, None, :] * eye2[None, None, :, None, :, None]).reshape(
            n_s, n_heads * 2 * dec_seq, HEAD_W)
        o_s = _attn_sample(lidx, page_table, qall, cache_k2, cache_v2,
                           k_rows.reshape(n_s, dec_seq * n_heads, HEAD_W),
                           v_rows.reshape(n_s, dec_seq * n_heads, HEAD_W),
                           lam_vecs, lam_init_rows, g_sub, slope_rows, dec_seq)
        y_attn_s = o_s.reshape(n_s, n_heads, dec_seq, HEAD_W).transpose(0, 2, 1, 3).reshape(
            m_s, w_attn)
        y_pool_s, p_st = _pool(lidx, zu[1], state_pool, True, w_pool, pool_scale, n_s, dec_seq,
                               past_len, F32)
        outs["ks"].append(k_rows)
        outs["vs"].append(v_rows)
        outs["cs"].append(c_st)
        outs["ps"].append(p_st)

        x = out_proj_ffn(lidx, x, ((y_conv_p, y_conv_s), (y_attn_p, y_attn_s),
                                   (y_pool_p, y_pool_s)), zg)

    y_prompt = _final_norm(x[0], g_final, tr).reshape(n_p, seq, d)
    y_sample = _final_norm(x[1], g_final, m_s).reshape(n_s, dec_seq, d)
    st = lambda k: jnp.stack(outs[k])
    return (y_prompt, y_sample, st("kp"), st("vp"), st("ks"), st("vs"),
            st("cp"), st("cs"), st("pp"), st("ps"))
```

```python
import functools
import math

import numpy as np
import jax
import jax.numpy as jnp
from jax import lax
from jax.experimental import pallas as pl
from jax.experimental.pallas import tpu as pltpu

F32 = jnp.float32
BF16 = jnp.bfloat16

EPS = 1e-6
HEAD_DIM = 128
HEAD_W = 2 * HEAD_DIM
CONV_W = 3
POOL_WINDOWS = (2, 4, 8, 16)
MAX_WIN = max(POOL_WINDOWS)
N_MOD = 6
MOD_SHIFT1, MOD_SCALE1, MOD_GATE1, MOD_SHIFT2, MOD_SCALE2, MOD_GATE2 = range(N_MOD)

V7X_VMEM_LIMIT = 56 * 1024 * 1024
SUBLANES = 8
LANES = 128


def _cparams(n_axes, vmem=V7X_VMEM_LIMIT):
    return pltpu.CompilerParams(dimension_semantics=("arbitrary",) * n_axes,
                                vmem_limit_bytes=vmem)


def _silu(x):
    return x * jax.nn.sigmoid(x)


def _log2(n):
    assert n > 0 and n & (n - 1) == 0, n
    return n.bit_length() - 1


def _fit(pref, *dims):
    g = 0
    for d in dims:
        g = math.gcd(g, d)
    best = None
    for t in range(LANES, min(pref, g) + 1, LANES):
        if g % t == 0:
            best = t
    assert best is not None, (pref, dims)
    return best


def _ada_kernel(c_ref, w_ref, b_ref, o_ref):
    s = _silu(c_ref[...]).astype(BF16)
    acc = jnp.dot(s, w_ref[...].astype(BF16), preferred_element_type=F32)
    o_ref[...] = acc + b_ref[...]


def _ada_mod(c_all, w_ada, b_ada):
    depth, d, n = w_ada.shape
    rows = c_all.shape[0]
    tn = _fit(512, n)
    return pl.pallas_call(
        _ada_kernel,
        grid=(depth, n // tn),
        in_specs=[pl.BlockSpec((rows, d), lambda l, j: (0, 0)),
                  pl.BlockSpec((None, d, tn), lambda l, j: (l, 0, j)),
                  pl.BlockSpec((None, 1, tn), lambda l, j: (l, 0, j))],
        out_specs=pl.BlockSpec((None, rows, tn), lambda l, j: (l, 0, j)),
        out_shape=jax.ShapeDtypeStruct((depth, rows, n), F32),
        compiler_params=_cparams(2),
        name="ada_mod",
    )(c_all, w_ada, b_ada.reshape(depth, 1, n))


class _Mod:
    def __init__(self, arr, rows_per_seq):
        self.arr = arr
        self.rows_per_seq = rows_per_seq

    def spec(self, which, tm, tn, row_of, col_of):
        if self.rows_per_seq is None:
            return pl.BlockSpec((None, None, tm, tn),
                                lambda *a: (a[-1][0], which, row_of(*a), col_of(*a)))
        tiles_per_seq = self.rows_per_seq // tm
        return pl.BlockSpec((None, None, None, 1, tn),
                            lambda *a: (a[-1][0], which, row_of(*a) // tiles_per_seq, 0, col_of(*a)))


def _norm_mod_kernel(l_ref, x_ref, g_ref, sc_ref, sh_ref, o_ref):
    x = x_ref[...]
    y = x * lax.rsqrt(jnp.mean(x * x, axis=-1, keepdims=True) + EPS) * g_ref[...]
    o_ref[...] = (y * (1.0 + sc_ref[...]) + sh_ref[...]).astype(o_ref.dtype)


def _norm_mod(lidx, x, g, mod, which_scale, which_shift, tr, out_dtype):
    m, d = x.shape
    depth = g.shape[0]
    row_of = lambda i, l: i
    col_of = lambda i, l: 0
    return pl.pallas_call(
        _norm_mod_kernel,
        grid_spec=pltpu.PrefetchScalarGridSpec(
            num_scalar_prefetch=1, grid=(m // tr,),
            in_specs=[pl.BlockSpec((tr, d), lambda i, l: (i, 0)),
                      pl.BlockSpec((None, 1, d), lambda i, l: (l[0], 0, 0)),
                      mod.spec(which_scale, tr, d, row_of, col_of),
                      mod.spec(which_shift, tr, d, row_of, col_of)],
            out_specs=pl.BlockSpec((tr, d), lambda i, l: (i, 0))),
        out_shape=jax.ShapeDtypeStruct((m, d), out_dtype),
        compiler_params=_cparams(1),
        name="norm_mod",
    )(lidx, x, g.reshape(depth, 1, d), mod.arr, mod.arr)


def _final_norm_kernel(x_ref, g_ref, o_ref):
    x = x_ref[...]
    o_ref[...] = x * lax.rsqrt(jnp.mean(x * x, axis=-1, keepdims=True) + EPS) * g_ref[...]


def _final_norm(x, g, tr):
    m, d = x.shape
    return pl.pallas_call(
        _final_norm_kernel,
        grid=(m // tr,),
        in_specs=[pl.BlockSpec((tr, d), lambda i: (i, 0)),
                  pl.BlockSpec((1, d), lambda i: (0, 0))],
        out_specs=pl.BlockSpec((tr, d), lambda i: (i, 0)),
        out_shape=jax.ShapeDtypeStruct((m, d), F32),
        compiler_params=_cparams(1),
        name="final_norm",
    )(x, g.reshape(1, d))


_MAX_W_CHUNKS = 8
_BF16_SUBLANES = 16


def _dense_kernel(*refs, n_groups, n_x, w_of_x, n_extra, epilogue, n_chunks, nj):
    n_w = len(w_of_x)
    refs = refs[1:]
    w_refs, refs = refs[:n_w], refs[n_w:]
    groups = []
    for _ in range(n_groups):
        groups.append((refs[:n_x], refs[n_x:n_x + n_extra]))
        refs = refs[n_x + n_extra:]
    o_refs, wb_refs = refs[:n_groups], refs[n_groups:]
    jj = pl.program_id(0)
    i = pl.program_id(1)

    @pl.when((jj < nj) & (i < n_chunks))
    def _():
        for w_ref, wb_ref in zip(w_refs, wb_refs):
            rows = w_ref.shape[0]
            start = pl.multiple_of(i * rows, rows)
            wb_ref[jj % 2, pl.ds(start, rows), :] = w_ref[...].astype(BF16)

    def compute(x_refs, extra, o_ref):
        slot = (jj - 1) % 2
        xb = [x_ref[...].astype(BF16) for x_ref in x_refs]
        acc = [jnp.dot(xb[w_of_x[k]], wb_refs[k][slot], preferred_element_type=F32)
               for k in range(n_w)]
        if epilogue == "plain":
            out = acc[0]
        elif epilogue == "swiglu":
            out = _silu(acc[0]) * acc[1]
        elif epilogue == "resid":
            out = extra[0][...] + extra[1][...] * acc[0]
        else:
            assert epilogue == "proj"
            out = jax.nn.sigmoid(extra[0][...].astype(F32)) * acc[0]
            for k in range(1, n_w):
                out += jax.nn.sigmoid(extra[k][...].astype(F32)) * acc[k]
        o_ref[...] = out.astype(o_ref.dtype)

    @pl.when(jj >= 1)
    def _():
        compute(*groups[0], o_refs[0])

    if n_groups == 2:
        @pl.when((jj >= 1) & (i == 0))
        def _():
            compute(*groups[1], o_refs[1])


def _dense(lidx, groups, ws, w_of_x, *, tn, n_cols, epilogue="plain", w_col0=0, which_gate=None):
    m = groups[0]["xs"][0][0].shape[0]
    ni = m // groups[0]["tm"]
    nj = pl.cdiv(n_cols, tn)
    n_chunks = min(ni, _MAX_W_CHUNKS)
    col = lambda jj, i, l: jnp.maximum(jj - 1, 0)

    in_specs, operands, scratch = [], [], []
    for arr, k_off, k in ws:
        kc = k // n_chunks
        assert k % n_chunks == 0 and kc % _BF16_SUBLANES == 0 and k_off % kc == 0, (k, k_off)
        assert w_col0 % tn == 0, (w_col0, tn)
        in_specs.append(pl.BlockSpec(
            (None, kc, tn),
            lambda jj, i, l, r0=k_off // kc: (l[0], r0 + jnp.minimum(i, n_chunks - 1),
                                             w_col0 // tn + jnp.minimum(jj, nj - 1))))
        operands.append(arr)
        scratch.append(pltpu.VMEM((2, k, tn), BF16))

    out_specs, out_shapes = [], []
    n_extra = {"plain": 0, "swiglu": 0, "resid": 2, "proj": len(groups[0].get("gates", ()))}[epilogue]
    for g, grp in enumerate(groups):
        tm = grp["tm"]
        mg = grp["xs"][0][0].shape[0]
        if g == 0:
            row = lambda jj, i, l: jnp.where(jj == 0, 0, i)
        else:
            assert mg == tm, (mg, tm)
            row = lambda jj, i, l: 0
        for arr, k, cb in grp["xs"]:
            in_specs.append(pl.BlockSpec(
                (tm, k), lambda jj, i, l, cb=cb, row=row: (row(jj, i, l), cb)))
            operands.append(arr)
        if epilogue == "resid":
            in_specs += [pl.BlockSpec((tm, tn), lambda jj, i, l, row=row: (row(jj, i, l),
                                                                           col(jj, i, l))),
                         grp["mod"].spec(which_gate, tm, tn, row, col)]
            operands += [grp["res"], grp["mod"].arr]
        elif epilogue == "proj":
            for arr, off in grp["gates"]:
                assert off % tn == 0, (off, tn)
                in_specs.append(pl.BlockSpec(
                    (tm, tn), lambda jj, i, l, b=off // tn, row=row: (row(jj, i, l),
                                                                      b + col(jj, i, l))))
                operands.append(arr)
        out_specs.append(pl.BlockSpec((tm, tn), lambda jj, i, l, row=row: (row(jj, i, l),
                                                                          col(jj, i, l))))
        out_shapes.append(jax.ShapeDtypeStruct((mg, n_cols), grp["out_dtype"]))

    return pl.pallas_call(
        functools.partial(_dense_kernel, n_groups=len(groups), n_x=len(groups[0]["xs"]),
                          w_of_x=tuple(w_of_x), n_extra=n_extra, epilogue=epilogue,
                          n_chunks=n_chunks, nj=nj),
        grid_spec=pltpu.PrefetchScalarGridSpec(
            num_scalar_prefetch=1, grid=(nj + 1, ni),
            in_specs=in_specs, out_specs=out_specs, scratch_shapes=scratch),
        out_shape=out_shapes,
        compiler_params=_cparams(2),
        name="dense_" + epilogue,
    )(lidx, *operands)


_CONV_BASE = SUBLANES


def _conv_kernel(l_ref, xa_ref, ba_ref, ca_ref, prev_ref, w_ref, y_ref, st_ref, ext_ref):
    t = xa_ref.shape[0]
    p = CONV_W - 1
    u = ca_ref[...].astype(F32) * xa_ref[...].astype(F32)
    ext_ref[pl.ds(_CONV_BASE - p, p), :] = prev_ref[...]
    ext_ref[pl.ds(_CONV_BASE, t), :] = u
    w = w_ref[...]
    y = u * w[CONV_W - 1:CONV_W, :]
    for j in range(CONV_W - 1):
        y += ext_ref[pl.ds(_CONV_BASE - p + j, t), :] * w[j:j + 1, :]
    y_ref[...] = (ba_ref[...].astype(F32) * y).astype(y_ref.dtype)
    st_ref[...] = ext_ref[pl.ds(_CONV_BASE + t - p, p), :]


def _conv(lidx, z, prev, prev_has_layers, conv_w, n_seq, t, w_conv, cols, tc, out_dtype):
    p = CONV_W - 1
    zs = lambda off: pl.BlockSpec((t, tc), lambda b, c, l: (b, off // tc + c))
    if prev_has_layers:
        prev_spec = pl.BlockSpec((None, None, p, tc), lambda b, c, l: (l[0], b, 0, c))
    else:
        prev_spec = pl.BlockSpec((None, None, p, tc), lambda b, c, l: (0, b, 0, c))
    return pl.pallas_call(
        _conv_kernel,
        grid_spec=pltpu.PrefetchScalarGridSpec(
            num_scalar_prefetch=1, grid=(n_seq, w_conv // tc),
            in_specs=[zs(cols[0]), zs(cols[1]), zs(cols[2]), prev_spec,
                      pl.BlockSpec((None, CONV_W, tc), lambda b, c, l: (l[0], 0, c))],
            out_specs=[pl.BlockSpec((t, tc), lambda b, c, l: (b, c)),
                       pl.BlockSpec((None, p, tc), lambda b, c, l: (b, 0, c))],
            scratch_shapes=[pltpu.VMEM((_CONV_BASE + t, tc), F32)]),
        out_shape=[jax.ShapeDtypeStruct((n_seq * t, w_conv), out_dtype),
                   jax.ShapeDtypeStruct((n_seq, p, w_conv), F32)],
        compiler_params=_cparams(2),
        name="short_conv",
    )(lidx, z, z, z, prev, conv_w)


_POOL_BASE = 2 * SUBLANES


def _pool_kernel(l_ref, u_ref, prev_ref, w_ref, sc_ref, y_ref, st_ref, ext_ref, *, start_pos):
    t = u_ref.shape[0]
    p = MAX_WIN - 1
    gc = w_ref.shape[1]
    ext_ref[pl.ds(_POOL_BASE - p, p), :] = prev_ref[...]
    ext_ref[pl.ds(_POOL_BASE, t), :] = u_ref[...]
    st_ref[...] = ext_ref[pl.ds(_POOL_BASE + t - p, p), :]
    pos = start_pos + lax.broadcasted_iota(jnp.int32, (t, 1), 0)
    for g, win in enumerate(POOL_WINDOWS):
        cols = slice(g * gc, (g + 1) * gc)
        u = ext_ref[pl.ds(_POOL_BASE, t), cols]
        win_sum = u
        for i in range(1, win):
            win_sum = win_sum + ext_ref[pl.ds(_POOL_BASE - i, t), cols]
        cnt = jnp.minimum(pos + 1, win).astype(F32)
        pooled = win_sum / cnt - u
        y = jnp.dot(pooled.astype(BF16), w_ref[g].astype(BF16), preferred_element_type=F32)
        y_ref[:, cols] = (y * sc_ref[:, cols]).astype(y_ref.dtype)


def _pool(lidx, u, prev, prev_has_layers, w_pool, pool_scale, n_seq, t, start_pos, out_dtype):
    depth, n_groups, gc, _ = w_pool.shape
    w_poolw = u.shape[1]
    p = MAX_WIN - 1
    if prev_has_layers:
        prev_spec = pl.BlockSpec((None, None, p, w_poolw), lambda b, l: (l[0], b, 0, 0))
    else:
        prev_spec = pl.BlockSpec((None, None, p, w_poolw), lambda b, l: (0, b, 0, 0))
    return pl.pallas_call(
        functools.partial(_pool_kernel, start_pos=start_pos),
        grid_spec=pltpu.PrefetchScalarGridSpec(
            num_scalar_prefetch=1, grid=(n_seq,),
            in_specs=[pl.BlockSpec((t, w_poolw), lambda b, l: (b, 0)),
                      prev_spec,
                      pl.BlockSpec((None, n_groups, gc, gc), lambda b, l: (l[0], 0, 0, 0)),
                      pl.BlockSpec((None, 1, w_poolw), lambda b, l: (l[0], 0, 0))],
            out_specs=[pl.BlockSpec((t, w_poolw), lambda b, l: (b, 0)),
                       pl.BlockSpec((None, p, w_poolw), lambda b, l: (b, 0, 0))],
            scratch_shapes=[pltpu.VMEM((_POOL_BASE + t, w_poolw), F32)]),
        out_shape=[jax.ShapeDtypeStruct((n_seq * t, w_poolw), out_dtype),
                   jax.ShapeDtypeStruct((n_seq, p, w_poolw), F32)],
        compiler_params=_cparams(1),
        name="ms_pool",
    )(lidx, u, prev, w_pool, pool_scale.reshape(depth, 1, w_poolw))


def _lam(lq1_ref, lk1_ref, lq2_ref, lk2_ref, lam_init):
    s1 = jnp.sum(lq1_ref[...] * lk1_ref[...], axis=-1, keepdims=True)
    s2 = jnp.sum(lq2_ref[...] * lk2_ref[...], axis=-1, keepdims=True)
    return jnp.exp(s1) - jnp.exp(s2) + lam_init


def _subln(o, g, lam_init):
    y = o * lax.rsqrt(jnp.mean(o * o, axis=-1, keepdims=True) + EPS) * g
    return y * (1.0 - lam_init)


def _attn_prompt_kernel(l_ref, q_ref, k_ref, v_ref, lq1_ref, lk1_ref, lq2_ref, lk2_ref,
                        li_ref, g_ref, slope_ref, o_ref, *, tq):
    qi = pl.program_id(2)
    scale = HEAD_DIM ** -0.5
    q = q_ref[...].astype(F32)
    qs = [(q[:, j * HEAD_DIM:(j + 1) * HEAD_DIM] * scale).astype(BF16) for j in range(2)]
    slope = slope_ref[:, 0:1]
    dist0 = (lax.broadcasted_iota(jnp.int32, (tq, tq), 0)
             - lax.broadcasted_iota(jnp.int32, (tq, tq), 1))

    def tile(ki, carry, diagonal):
        start = pl.multiple_of(ki * tq, tq)
        kt = k_ref[pl.ds(start, tq), :]
        vt = v_ref[pl.ds(start, tq), :].astype(BF16)
        dist = (dist0 + (qi - ki) * tq).astype(F32)
        bias = slope * dist
        out = []
        for j in range(2):
            m, l, acc = carry[j]
            kj = kt[:, j * HEAD_DIM:(j + 1) * HEAD_DIM].astype(BF16)
            s = lax.dot_general(qs[j], kj, (((1,), (1,)), ((), ())),
                                preferred_element_type=F32) - bias
            if diagonal:
                s = jnp.where(dist >= 0, s, -jnp.inf)
            m_new = jnp.maximum(m, jnp.max(s, axis=-1, keepdims=True))
            alpha = jnp.exp(m - m_new)
            pm = jnp.exp(s - m_new)
            l = alpha * l + jnp.sum(pm, axis=-1, keepdims=True)
            acc = alpha * acc + jnp.dot(pm.astype(BF16), vt, preferred_element_type=F32)
            out.append((m_new, l, acc))
        return tuple(out)

    init = tuple((jnp.full((tq, 1), -jnp.inf, F32), jnp.zeros((tq, 1), F32),
                  jnp.zeros((tq, HEAD_W), F32)) for _ in range(2))
    carry = lax.fori_loop(0, qi // 2,
                          lambda kp, c: tile(2 * kp + 1, tile(2 * kp, c, False), False), init)
    carry = lax.cond(qi % 2 == 1, lambda c: tile(qi - 1, c, False), lambda c: c, carry)
    (_, l1, a1), (_, l2, a2) = tile(qi, carry, True)

    lam_init = li_ref[:, 0:1]
    lam = _lam(lq1_ref, lk1_ref, lq2_ref, lk2_ref, lam_init)
    o = a1 / l1 - lam * (a2 / l2)
    o_ref[...] = _subln(o, g_ref[...], lam_init).astype(o_ref.dtype)


def _small_specs(n_grid):
    lay = lambda width: pl.BlockSpec((None, 1, width), lambda *a: (a[n_grid][0], 0, 0))
    return [lay(HEAD_DIM)] * 4 + [lay(HEAD_DIM), lay(HEAD_W)]


def _attn_prompt(lidx, zq, q_col, zk, zv, lam_vecs, lam_init_rows, g_subln, slopes,
                 n_seq, t, n_heads, tq, out_dtype):
    nq = t // tq
    qb = q_col // HEAD_W
    kv_spec = pl.BlockSpec((t, HEAD_W), lambda b, h, i, l: (b, h))
    return pl.pallas_call(
        functools.partial(_attn_prompt_kernel, tq=tq),
        grid_spec=pltpu.PrefetchScalarGridSpec(
            num_scalar_prefetch=1, grid=(n_seq, n_heads, nq),
            in_specs=[pl.BlockSpec((tq, HEAD_W), lambda b, h, i, l: (b * nq + i, qb + h)),
                      kv_spec, kv_spec]
                     + _small_specs(3)
                     + [pl.BlockSpec((None, 1, HEAD_DIM), lambda b, h, i, l: (h, 0, 0))],
            out_specs=pl.BlockSpec((tq, HEAD_W), lambda b, h, i, l: (b * nq + i, h))),
        out_shape=jax.ShapeDtypeStruct((n_seq * t, n_heads * HEAD_W), out_dtype),
        compiler_params=_cparams(3),
        name="attn_prompt",
    )(lidx, zq, zk, zv, *lam_vecs, lam_init_rows, g_subln, slopes)


_PAGES_PER_STEP = 4


def _attn_sample_kernel(l_ref, pt_ref, q_ref, *refs, n_pages, page, n_heads, t_new):
    kv_refs, refs = refs[:2 * _PAGES_PER_STEP], refs[2 * _PAGES_PER_STEP:]
    (kn_ref, vn_ref, lq1_ref, lk1_ref, lq2_ref, lk2_ref, li_ref, g_ref, slope_ref,
     o_ref, m_ref, ls_ref, acc_ref, bias_ref) = refs
    p = pl.program_id(1)
    n_steps = n_pages // _PAGES_PER_STEP
    scale = HEAD_DIM ** -0.5
    n_rows = q_ref.shape[0]
    per_head = 2 * t_new
    slope = slope_ref[...]
    past = n_pages * page

    def ids(n_keys):
        r = lax.broadcasted_iota(jnp.int32, (n_rows, n_keys), 0)
        c = lax.broadcasted_iota(jnp.int32, (n_rows, n_keys), 1)
        return (r >> _log2(per_head), r & (t_new - 1), c >> _log2(n_heads), c & (n_heads - 1))

    @pl.when(p == 0)
    def _():
        m_ref[...] = jnp.full(m_ref.shape, -jnp.inf, F32)
        ls_ref[...] = jnp.zeros(ls_ref.shape, F32)
        acc_ref[...] = jnp.zeros(acc_ref.shape, F32)
        rhead, tok, kpos, khead = ids(page * n_heads)
        dist = (past + tok - kpos).astype(F32)
        bias_ref[...] = jnp.where(rhead == khead, -slope * dist, -jnp.inf)

    qb = (q_ref[...] * scale).astype(BF16)

    def update(kb, vb, bias, c_row):
        s = lax.dot_general(qb, kb, (((1,), (1,)), ((), ())), preferred_element_type=F32) + bias
        m_old = m_ref[...]
        m_new = jnp.maximum(m_old, jnp.max(s, axis=-1, keepdims=True) + c_row)
        alpha = jnp.exp(m_old - m_new)
        pm = jnp.exp(s - (m_new - c_row))
        ls_ref[...] = alpha * ls_ref[...] + jnp.sum(pm, axis=-1, keepdims=True)
        acc_ref[...] = alpha * acc_ref[...] + jnp.dot(pm.astype(BF16), vb,
                                                     preferred_element_type=F32)
        m_ref[...] = m_new

    @pl.when(p < n_steps)
    def _():
        for u in range(_PAGES_PER_STEP):
            first_pos = jnp.full((1, 1), (p * _PAGES_PER_STEP + u) * page, jnp.int32).astype(F32)
            update(kv_refs[2 * u][...].astype(BF16), kv_refs[2 * u + 1][...].astype(BF16),
                   bias_ref[...], slope * first_pos)

    @pl.when(p == n_steps)
    def _():
        rhead, tok, kpos, khead = ids(t_new * n_heads)
        valid = (rhead == khead) & (kpos <= tok)
        bias = jnp.where(valid, -slope * (tok - kpos).astype(F32), -jnp.inf)
        update(kn_ref[...].astype(BF16), vn_ref[...].astype(BF16), bias,
               jnp.zeros((1, 1), F32))

        lam_init = li_ref[:, 0:1]
        lam = _lam(lq1_ref, lk1_ref, lq2_ref, lk2_ref, lam_init)
        acc = acc_ref[...] / ls_ref[...]
        a3 = acc.reshape(n_heads, per_head, HEAD_W)
        o1 = a3[:, 0:t_new, :].reshape(n_heads * t_new, HEAD_W)
        o2 = a3[:, t_new:per_head, :].reshape(n_heads * t_new, HEAD_W)
        o_ref[...] = _subln(o1 - lam * o2, g_ref[...], lam_init)


def _attn_sample(lidx, page_table, qall, cache_k, cache_v, k_new, v_new, lam_vecs, lam_init_rows,
                 g_subln, slope_rows, t_new):
    n_seq, n_pages = page_table.shape
    keys = cache_k.shape[2]
    n_heads = k_new.shape[1] // t_new
    page = keys // n_heads
    n_rows = qall.shape[1]
    assert n_pages % _PAGES_PER_STEP == 0
    n_steps = n_pages // _PAGES_PER_STEP

    def cache_spec(u):
        return pl.BlockSpec(
            (None, None, keys, HEAD_W),
            lambda b, p, l, pt: (l[0], pt[b * n_pages + jnp.minimum(p, n_steps - 1)
                                          * _PAGES_PER_STEP + u], 0, 0))

    cache_specs, cache_args = [], []
    for u in range(_PAGES_PER_STEP):
        cache_specs += [cache_spec(u), cache_spec(u)]
        cache_args += [cache_k, cache_v]
    per_seq = lambda r, c: pl.BlockSpec((None, r, c), lambda b, p, l, pt: (b, 0, 0))
    return pl.pallas_call(
        functools.partial(_attn_sample_kernel, n_pages=n_pages, page=page, n_heads=n_heads,
                          t_new=t_new),
        grid_spec=pltpu.PrefetchScalarGridSpec(
            num_scalar_prefetch=2, grid=(n_seq, n_steps + 1),
            in_specs=[per_seq(n_rows, HEAD_W)] + cache_specs
                     + [per_seq(t_new * n_heads, HEAD_W), per_seq(t_new * n_heads, HEAD_W)]
                     + _small_specs(2)
                     + [pl.BlockSpec((n_rows, 1), lambda b, p, l, pt: (0, 0))],
            out_specs=per_seq(n_heads * t_new, HEAD_W),
            scratch_shapes=[pltpu.VMEM((n_rows, 1), F32), pltpu.VMEM((n_rows, 1), F32),
                            pltpu.VMEM((n_rows, HEAD_W), F32),
                            pltpu.VMEM((n_rows, keys), F32)]),
        out_shape=jax.ShapeDtypeStruct((n_seq, n_heads * t_new, HEAD_W), F32),
        compiler_params=_cparams(2),
        name="attn_sample",
    )(lidx, page_table.reshape(-1), qall, *cache_args, k_new, v_new, *lam_vecs,
      lam_init_rows, g_subln, slope_rows)


def kernel(x_prompt, x_sample, cache_k, cache_v, state_conv, state_pool, page_table, c_prompt,
           c_sample, w_ada, b_ada, g_norm1, w_in, conv_w, lam_q1, lam_k1, lam_q2, lam_k2, g_subln,
           w_pool, pool_scale, w_proj_conv, w_proj_attn, w_proj_pool, w_out, g_norm2, w_ffn_gate,
           w_ffn_up, w_ffn_down, g_final):
    n_p, seq, d = x_prompt.shape
    n_s, dec_seq, _ = x_sample.shape
    depth = w_ada.shape[0]
    w_conv = conv_w.shape[2]
    w_poolw = pool_scale.shape[1]
    n_heads = cache_k.shape[3]
    w_attn = n_heads * HEAD_W
    d_ff = w_ffn_gate.shape[2]
    page = cache_k.shape[2]
    n_pages = page_table.shape[1]
    past_len = n_pages * page
    m_p, m_s = n_p * seq, n_s * dec_seq

    wa = 3 * w_conv + w_attn
    off_k, off_v, off_u, off_g = wa, wa + w_attn, wa + 2 * w_attn, wa + 2 * w_attn + w_poolw
    c_xa, c_ba, c_ca, c_q = 0, w_conv, 2 * w_conv, 3 * w_conv

    tm_big, tm_mid = min(1024, m_p), min(512, m_p)
    tn_d = _fit(1024, d)
    tn_ff = 512
    half_ff = d_ff // 2
    assert d_ff % 2 == 0 and half_ff % LANES == 0
    tc = _fit(256, w_conv)
    tq = min(256, seq)
    tr = min(512, seq)

    n_c = n_p + n_s
    pad = (-n_c) % SUBLANES
    c_all = jnp.concatenate([c_prompt, c_sample, jnp.zeros((pad, d), F32)], axis=0)
    mod = _ada_mod(c_all, w_ada, b_ada)
    mod_p = _Mod(mod[:, :n_p].reshape(depth, n_p, N_MOD, 1, d).transpose(0, 2, 1, 3, 4), seq)
    mod_s_rows = jnp.repeat(mod[:, n_p:n_c].reshape(depth, n_s, N_MOD, d).transpose(0, 2, 1, 3),
                            dec_seq, axis=2)
    mod_s = _Mod(mod_s_rows, None)

    slopes_np = 2.0 ** (-8.0 * np.arange(1, n_heads + 1) / n_heads)
    slopes = jnp.asarray(np.broadcast_to(slopes_np[:, None, None], (n_heads, 1, HEAD_DIM)), F32)
    slope_rows = jnp.asarray(np.repeat(slopes_np, 2 * dec_seq)[:, None], F32)
    lam_init_np = np.array([0.8 - 0.6 * math.exp(-0.3 * l) for l in range(depth)])
    lam_init_rows = jnp.asarray(np.broadcast_to(lam_init_np[:, None, None], (depth, 1, HEAD_DIM)),
                                F32)
    lam_vecs = [v.reshape(depth, 1, HEAD_DIM) for v in (lam_q1, lam_k1, lam_q2, lam_k2)]
    g_sub = g_subln.reshape(depth, 1, HEAD_W)

    conv0 = jnp.zeros((1, n_p, CONV_W - 1, w_conv), F32)
    pool0 = jnp.zeros((1, n_p, MAX_WIN - 1, w_poolw), F32)
    cache_k2 = cache_k.reshape(depth, cache_k.shape[1], page * n_heads, HEAD_W)
    cache_v2 = cache_v.reshape(depth, cache_v.shape[1], page * n_heads, HEAD_W)
    eye2 = jnp.eye(2, dtype=F32)

    def both(xs_p, xs_s, tm, dt_p, dt_s, **kw):
        return [dict(xs=xs_p, tm=tm, out_dtype=dt_p, **{k: v[0] for k, v in kw.items()}),
                dict(xs=xs_s, tm=m_s, out_dtype=dt_s, **{k: v[1] for k, v in kw.items()})]

    def in_proj(lidx, h_p, h_s):
        w = [(w_in, 0, d)]
        mk = lambda n, c0, dt_p, unit: _dense(
            lidx, both([(h_p, d, 0)], [(h_s, d, 0)], tm_big, dt_p, F32), w, (0,),
            tn=_fit(1024, unit, c0), n_cols=n, w_col0=c0)
        return (mk(wa, 0, BF16, wa), mk(w_attn, off_k, F32, w_attn), mk(w_attn, off_v, F32, w_attn),
                mk(w_poolw, off_u, F32, w_poolw), mk(3 * d, off_g, BF16, d))

    def out_proj_ffn(lidx, x, ys, zg):
        y_xs = lambda g: [(ys[0][g], w_conv, 0), (ys[1][g], w_attn, 0), (ys[2][g], w_poolw, 0)]
        gate_cols = lambda g: [(zg[g], 0), (zg[g], d), (zg[g], 2 * d)]
        merged = _dense(lidx, both(y_xs(0), y_xs(1), tm_mid, BF16, BF16,
                                   gates=(gate_cols(0), gate_cols(1))),
                        [(w_proj_conv, 0, w_conv), (w_proj_attn, 0, w_attn),
                         (w_proj_pool, 0, w_poolw)], (0, 1, 2), tn=tn_d, n_cols=d, epilogue="proj")
        x = _dense(lidx, both([(merged[0], d, 0)], [(merged[1], d, 0)], tm_mid, F32, F32,
                              res=x, mod=(mod_p, mod_s)),
                   [(w_out, 0, d)], (0,), tn=tn_d, n_cols=d, epilogue="resid",
                   which_gate=MOD_GATE1)
        h2 = (_norm_mod(lidx, x[0], g_norm2, mod_p, MOD_SCALE2, MOD_SHIFT2, tr, BF16),
              _norm_mod(lidx, x[1], g_norm2, mod_s, MOD_SCALE2, MOD_SHIFT2, m_s, BF16))
        act = _dense(lidx, both([(h2[0], d, 0)], [(h2[1], d, 0)], tm_big, BF16, BF16),
                     [(w_ffn_gate, 0, d), (w_ffn_up, 0, d)], (0, 0), tn=tn_ff, n_cols=d_ff,
                     epilogue="swiglu")
        for half in range(2):
            x = _dense(lidx, both([(act[0], half_ff, half)], [(act[1], half_ff, half)], tm_mid,
                                  F32, F32, res=x, mod=(mod_p, mod_s)),
                       [(w_ffn_down, half * half_ff, half_ff)], (0,), tn=tn_d, n_cols=d,
                       epilogue="resid", which_gate=MOD_GATE2)
        return x

    x = (x_prompt.reshape(m_p, d), x_sample.reshape(m_s, d))
    outs = {k: [] for k in ("kp", "vp", "ks", "vs", "cp", "cs", "pp", "ps")}

    for layer in range(depth):
        lidx = jnp.full((1,), layer, jnp.int32)
        h_p = _norm_mod(lidx, x[0], g_norm1, mod_p, MOD_SCALE1, MOD_SHIFT1, tr, BF16)
        h_s = _norm_mod(lidx, x[1], g_norm1, mod_s, MOD_SCALE1, MOD_SHIFT1, m_s, BF16)
        za, zk, zv, zu, zg = in_proj(lidx, h_p, h_s)

        y_conv_p, c_st = _conv(lidx, za[0], conv0, False, conv_w, n_p, seq, w_conv,
                               (c_xa, c_ba, c_ca), tc, BF16)
        y_attn_p = _attn_prompt(lidx, za[0], c_q, zk[0], zv[0], lam_vecs, lam_init_rows,
                                g_sub, slopes, n_p, seq, n_heads, tq, BF16)
        y_pool_p, p_st = _pool(lidx, zu[0], pool0, False, w_pool, pool_scale, n_p, seq, 0, BF16)
        outs["kp"].append(zk[0].reshape(n_p, seq, n_heads, HEAD_W))
        outs["vp"].append(zv[0].reshape(n_p, seq, n_heads, HEAD_W))
        outs["cp"].append(c_st)
        outs["pp"].append(p_st)

        y_conv_s, c_st = _conv(lidx, za[1], state_conv, True, conv_w, n_s, dec_seq, w_conv,
                               (c_xa, c_ba, c_ca), tc, F32)
        k_rows = zk[1].reshape(n_s, dec_seq, n_heads, HEAD_W)
        v_rows = zv[1].reshape(n_s, dec_seq, n_heads, HEAD_W)
        q5 = za[1][:, c_q:c_q + w_attn].reshape(n_s, dec_seq, n_heads, 2, HEAD_DIM)
        qh = q5.transpose(0, 2, 3, 1, 4)
        qall = (qh[:, :, :, :, None, :] * eye2[None, None, :, None, :, None]).reshape(
            n_s, n_heads * 2 * dec_seq, HEAD_W)
        o_s = _attn_sample(lidx, page_table, qall, cache_k2, cache_v2,
                           k_rows.reshape(n_s, dec_seq * n_heads, HEAD_W),
                           v_rows.reshape(n_s, dec_seq * n_heads, HEAD_W),
                           lam_vecs, lam_init_rows, g_sub, slope_rows, dec_seq)
        y_attn_s = o_s.reshape(n_s, n_heads, dec_seq, HEAD_W).transpose(0, 2, 1, 3).reshape(
            m_s, w_attn)
        y_pool_s, p_st = _pool(lidx, zu[1], state_pool, True, w_pool, pool_scale, n_s, dec_seq,
                               past_len, F32)
        outs["ks"].append(k_rows)
        outs["vs"].append(v_rows)
        outs["cs"].append(c_st)
        outs["ps"].append(p_st)

        x = out_proj_ffn(lidx, x, ((y_conv_p, y_conv_s), (y_attn_p, y_attn_s),
                                   (y_pool_p, y_pool_s)), zg)

    y_prompt = _final_norm(x[0], g_final, tr).reshape(n_p, seq, d)
    y_sample = _final_norm(x[1], g_final, m_s).reshape(n_s, dec_seq, d)
    st = lambda k: jnp.stack(outs[k])
    return (y_prompt, y_sample, st("kp"), st("vp"), st("ks"), st("vs"),
            st("cp"), st("cs"), st("pp"), st("ps"))
```
